```python
import math
import jax, jax.numpy as jnp
from jax import lax
import numpy as np

D_MODEL = 1024
BATCH = 4
SEQ = 4096
DEPTH = 2

CHUNK = 64
N_MIXERS = 2
N_S5 = (DEPTH + 1) // 2
N_GLA = DEPTH // 2
S5_GROUP = 16
S5_GROUPS = D_MODEL // S5_GROUP
S5_STATE = 64
DT_MIN = 1e-3
DT_MAX = 1e-1
GLA_HEADS = 4
GLA_QK = D_MODEL // 2
GLA_DK = GLA_QK // GLA_HEADS
GLA_DV = D_MODEL // GLA_HEADS
GLA_GATE_RANK = 16
GLA_GATE_TAU = 16.0
GLA_IN = 2 * GLA_QK + D_MODEL + GLA_GATE_RANK + D_MODEL
D_FF = 4 * D_MODEL
EPS = 1e-6

kernel_name = "chunk_causal_s5_gla_hybrid"


def rmsnorm(x, g):
    xf = x.astype(jnp.float32)
    y = xf * lax.rsqrt(jnp.mean(xf * xf, axis=-1, keepdims=True) + EPS) * g.astype(jnp.float32)
    return y.astype(x.dtype)


def modulate(h, shift, scale):
    return h * (1.0 + scale[:, None, :]) + shift[:, None, :]


def s5_mixer(u, a_re, a_im, log_dt, b_re, b_im, c_re, c_im, d_skip, w_glu):
    bsz, seq, _ = u.shape
    f32 = jnp.float32
    uf = u.astype(f32).reshape(bsz, seq, S5_GROUPS, S5_GROUP)
    dt = jnp.exp(log_dt.astype(f32))[:, None]
    ar = a_re.astype(f32)
    ai = a_im.astype(f32)
    mag = jnp.exp(ar * dt)
    ph = ai * dt
    lb_re = mag * jnp.cos(ph)
    lb_im = mag * jnp.sin(ph)
    den = ar * ar + ai * ai
    nr = lb_re - 1.0
    ni = lb_im
    f_re = (nr * ar + ni * ai) / den
    f_im = (ni * ar - nr * ai) / den
    br = b_re.astype(f32)
    bi = b_im.astype(f32)
    bb_re = f_re[..., None] * br - f_im[..., None] * bi
    bb_im = f_re[..., None] * bi + f_im[..., None] * br
    bu_re = jnp.einsum('blgh,gph->blgp', uf, bb_re)
    bu_im = jnp.einsum('blgh,gph->blgp', uf, bb_im)
    la_re = jnp.broadcast_to(lb_re, bu_re.shape)
    la_im = jnp.broadcast_to(lb_im, bu_im.shape)

    def combine(left, right):
        a1r, a1i, b1r, b1i = left
        a2r, a2i, b2r, b2i = right
        return (a2r * a1r - a2i * a1i,
                a2r * a1i + a2i * a1r,
                a2r * b1r - a2i * b1i + b2r,
                a2r * b1i + a2i * b1r + b2i)

    _, _, xr, xi = lax.associative_scan(combine, (la_re, la_im, bu_re, bu_im), axis=1)
    y = (jnp.einsum('blgp,ghp->blgh', xr, c_re.astype(f32))
         - jnp.einsum('blgp,ghp->blgh', xi, c_im.astype(f32))
         + d_skip.astype(f32).reshape(S5_GROUPS, S5_GROUP) * uf)
    z = jax.nn.gelu(y.reshape(bsz, seq, D_MODEL)).astype(u.dtype)
    val, gate = jnp.split(z @ w_glu, 2, axis=-1)
    return val * jax.nn.sigmoid(gate)


def gla_mixer(h, w_in, w_gate2, b_gate, g_norm, w_out):
    bsz, seq, _ = h.shape
    n = seq // CHUNK
    f32 = jnp.float32
    proj = h @ w_in
    q, k, v, glr, r = jnp.split(
        proj, [GLA_QK, 2 * GLA_QK, 2 * GLA_QK + D_MODEL, 2 * GLA_QK + D_MODEL + GLA_GATE_RANK], axis=-1)
    log_a = jax.nn.log_sigmoid((glr @ w_gate2 + b_gate).astype(f32)) / GLA_GATE_TAU

    def heads(t, dh):
        return t.reshape(bsz, n, CHUNK, GLA_HEADS, dh).transpose(0, 3, 1, 2, 4).astype(f32)

    q = heads(q, GLA_DK) * (GLA_DK ** -0.5)
    k = heads(k, GLA_DK)
    v = heads(v, GLA_DV)
    gc = jnp.cumsum(heads(log_a, GLA_DK), axis=3)
    g_end = gc[:, :, :, -1:, :]
    k_dec = k * jnp.exp(g_end - gc)
    scores = jnp.einsum('bhncd,bhnsd->bhncs', q, k_dec)
    o_intra = jnp.einsum('bhncs,bhnse->bhnce', scores, v)
    kv = jnp.einsum('bhnsd,bhnse->bhnde', k_dec, v)
    decay = jnp.exp(g_end[:, :, :, 0, :])

    def step(state, inp):
        kv_c, dec_c = inp
        return dec_c[..., None] * state + kv_c, state

    s0 = jnp.zeros((bsz, GLA_HEADS, GLA_DK, GLA_DV), f32)
    _, s_prev = lax.scan(step, s0, (kv.transpose(2, 0, 1, 3, 4), decay.transpose(2, 0, 1, 3)))
    s_prev = s_prev.transpose(1, 2, 0, 3, 4)
    o_inter = jnp.einsum('bhncd,bhnde->bhnce', q * jnp.exp(g_end), s_prev)
    o = o_intra + o_inter
    o = o * lax.rsqrt(jnp.mean(o * o, axis=-1, keepdims=True) + EPS)
    o = o.transpose(0, 2, 3, 1, 4).reshape(bsz, seq, D_MODEL) * g_norm.astype(f32)
    o = o.astype(h.dtype) * jax.nn.silu(r)
    return o @ w_out


def sqrelu_mlp(h, w1, w2):
    a = jax.nn.relu(h @ w1)
    return (a * a) @ w2


def setup_inputs(seed: int = 0) -> dict:
    key = jax.random.key(seed)
    ks = jax.random.split(key, 24)
    f32 = jnp.float32
    nrm = lambda k, shape, s: jax.random.normal(k, shape, f32) * s
    x = jax.random.normal(ks[0], (BATCH, SEQ, D_MODEL), f32)
    c = jax.random.normal(ks[1], (BATCH, D_MODEL), f32)
    w_ada = nrm(ks[2], (DEPTH, D_MODEL, 6 * D_MODEL), 0.5 * D_MODEL ** -0.5)
    b_ada = nrm(ks[3], (DEPTH, 6 * D_MODEL), 0.02)
    norm_mix = 1.0 + nrm(ks[4], (DEPTH, D_MODEL), 0.02)
    norm_mlp = 1.0 + nrm(ks[5], (DEPTH, D_MODEL), 0.02)
    n_idx = jnp.arange(S5_STATE, dtype=f32)
    s5_a_re = -0.5 * jnp.exp(nrm(ks[6], (N_S5, S5_GROUPS, S5_STATE), 0.05))
    s5_a_im = math.pi * n_idx + nrm(ks[7], (N_S5, S5_GROUPS, S5_STATE), 0.05)
    s5_log_dt = jax.random.uniform(ks[8], (N_S5, S5_GROUPS), f32, math.log(DT_MIN), math.log(DT_MAX))
    s5_b_re = nrm(ks[9], (N_S5, S5_GROUPS, S5_STATE, S5_GROUP), (2.0 * S5_GROUP) ** -0.5)
    s5_b_im = nrm(ks[10], (N_S5, S5_GROUPS, S5_STATE, S5_GROUP), (2.0 * S5_GROUP) ** -0.5)
    s5_c_re = nrm(ks[11], (N_S5, S5_GROUPS, S5_GROUP, S5_STATE), S5_STATE ** -0.5)
    s5_c_im = nrm(ks[12], (N_S5, S5_GROUPS, S5_GROUP, S5_STATE), S5_STATE ** -0.5)
    s5_d = nrm(ks[13], (N_S5, D_MODEL), 1.0)
    s5_w_glu = nrm(ks[14], (N_S5, D_MODEL, 2 * D_MODEL), D_MODEL ** -0.5)
    gla_w_in = nrm(ks[15], (N_GLA, D_MODEL, GLA_IN), D_MODEL ** -0.5)
    gla_w_gate2 = nrm(ks[16], (N_GLA, GLA_GATE_RANK, GLA_QK), GLA_GATE_RANK ** -0.5)
    gla_b_gate = nrm(ks[17], (N_GLA, GLA_QK), 0.1)
    gla_g_norm = 1.0 + nrm(ks[18], (N_GLA, D_MODEL), 0.02)
    gla_w_out = nrm(ks[19], (N_GLA, D_MODEL, D_MODEL), D_MODEL ** -0.5)
    w_ff1 = nrm(ks[20], (DEPTH, D_MODEL, D_FF), D_MODEL ** -0.5)
    w_ff2 = nrm(ks[21], (DEPTH, D_FF, D_MODEL), D_FF ** -0.5)
    norm_final = 1.0 + nrm(ks[22], (D_MODEL,), 0.02)
    return {"x": x, "c": c, "w_ada": w_ada, "b_ada": b_ada, "norm_mix": norm_mix, "norm_mlp": norm_mlp,
            "s5_a_re": s5_a_re, "s5_a_im": s5_a_im, "s5_log_dt": s5_log_dt, "s5_b_re": s5_b_re,
            "s5_b_im": s5_b_im, "s5_c_re": s5_c_re, "s5_c_im": s5_c_im, "s5_d": s5_d, "s5_w_glu": s5_w_glu,
            "gla_w_in": gla_w_in, "gla_w_gate2": gla_w_gate2, "gla_b_gate": gla_b_gate,
            "gla_g_norm": gla_g_norm, "gla_w_out": gla_w_out, "w_ff1": w_ff1, "w_ff2": w_ff2,
            "norm_final": norm_final}


def reference(x, c, w_ada, b_ada, norm_mix, norm_mlp, s5_a_re, s5_a_im, s5_log_dt, s5_b_re, s5_b_im,
              s5_c_re, s5_c_im, s5_d, s5_w_glu, gla_w_in, gla_w_gate2, gla_b_gate, gla_g_norm, gla_w_out,
              w_ff1, w_ff2, norm_final):
    cs = jax.nn.silu(c)
    for i in range(DEPTH):
        mod = cs @ w_ada[i] + b_ada[i]
        sh1, sc1, gt1, sh2, sc2, gt2 = jnp.split(mod, 6, axis=-1)
        h = modulate(rmsnorm(x, norm_mix[i]), sh1, sc1)
        j = i // N_MIXERS
        if i % N_MIXERS == 0:
            y = s5_mixer(h, s5_a_re[j], s5_a_im[j], s5_log_dt[j], s5_b_re[j], s5_b_im[j],
                         s5_c_re[j], s5_c_im[j], s5_d[j], s5_w_glu[j])
        else:
            y = gla_mixer(h, gla_w_in[j], gla_w_gate2[j], gla_b_gate[j], gla_g_norm[j], gla_w_out[j])
        x = x + gt1[:, None, :] * y
        h = modulate(rmsnorm(x, norm_mlp[i]), sh2, sc2)
        x = x + gt2[:, None, :] * sqrelu_mlp(h, w_ff1[i], w_ff2[i])
    return rmsnorm(x, norm_final)
```

```python
import functools
import math

import jax
import jax.numpy as jnp
from jax import lax
from jax.experimental import pallas as pl
from jax.experimental.pallas import tpu as pltpu

F32 = jnp.float32
BF16 = jnp.bfloat16

D_MODEL = 1024
DEPTH = 2
CHUNK = 64
S5_GROUP = 16
S5_GROUPS = D_MODEL // S5_GROUP
S5_STATE = 64
GLA_HEADS = 4
GLA_QK = D_MODEL // 2
GLA_DK = GLA_QK // GLA_HEADS
GLA_DV = D_MODEL // GLA_HEADS
GLA_GATE_RANK = 16
GLA_GATE_TAU = 16.0
D_FF = 4 * D_MODEL
EPS = 1e-6

LANES = 128
SUBLANES = 8
VMEM_LIMIT_BYTES = 56 * 1024 * 1024

S5_SLABS = D_MODEL // LANES
S5_SLAB_GROUPS = LANES // S5_GROUP
S5_SLAB_COLS = S5_SLAB_GROUPS * 2 * S5_STATE
S5_OCTETS = S5_SLABS // 2
S5_OCTET_COLS = 2 * S5_SLAB_COLS
S5_TILES = S5_OCTET_COLS // (2 * LANES)
S5_COLS = S5_SLABS * S5_SLAB_COLS

S5_TM = 128
MLP_TM = 512
GLA_TM = 256

GLA_GLR_PAD = LANES
GLA_IN_PAD = 2 * GLA_QK + 2 * D_MODEL + GLA_GLR_PAD


def _cparams(sem):
    return pltpu.CompilerParams(dimension_semantics=sem, vmem_limit_bytes=VMEM_LIMIT_BYTES)


def _const_spec(shape):
    nd = len(shape)
    return pl.BlockSpec(shape, lambda *_: (0,) * nd, pipeline_mode=pl.Buffered(1))


def _rms(x, g):
    return x * lax.rsqrt(jnp.mean(x * x, axis=-1, keepdims=True) + EPS) * g


def _adaln_kernel(c_ref, w_ref, b_ref, o_ref):
    c = c_ref[...]
    cs = (c * jax.nn.sigmoid(c)).astype(BF16)
    o_ref[0] = jnp.dot(cs, w_ref[0].astype(BF16), preferred_element_type=F32) + b_ref[0]


def _adaln(c, w_ada, b_ada):
    nb = c.shape[0]
    bsz = -(-nb // SUBLANES) * SUBLANES
    c = jnp.pad(c, ((0, bsz - nb), (0, 0)))
    n = w_ada.shape[-1]
    tn = 1536
    out = pl.pallas_call(
        _adaln_kernel,
        grid=(DEPTH, n // tn),
        in_specs=[
            pl.BlockSpec((bsz, D_MODEL), lambda i, j: (0, 0)),
            pl.BlockSpec((1, D_MODEL, tn), lambda i, j: (i, 0, j)),
            pl.BlockSpec((1, 1, tn), lambda i, j: (i, 0, j)),
        ],
        out_specs=pl.BlockSpec((1, bsz, tn), lambda i, j: (i, 0, j)),
        out_shape=jax.ShapeDtypeStruct((DEPTH, bsz, n), F32),
        compiler_params=_cparams(("arbitrary", "arbitrary")),
        name="adaln",
    )(c, w_ada, b_ada.reshape(DEPTH, 1, n))
    return out[:, :nb]


def _s5_prep_kernel(ar_ref, ai_ref, ldt_ref, br_ref, bi_ref, cr_ref, ci_ref,
                    lam2r_ref, lam2i_ref, bbr_ref, bbi_ref, lbr_ref, lbi_ref, cro_ref, cio_ref):
    dt = jnp.exp(ldt_ref[...])
    ar = ar_ref[...]
    ai = ai_ref[...]
    mag = jnp.exp(ar * dt)
    ph = ai * dt
    lre = mag * jnp.cos(ph)
    lim = mag * jnp.sin(ph)
    den = ar * ar + ai * ai
    nr = lre - 1.0
    ni = lim
    fre = (nr * ar + ni * ai) / den
    fim = (ni * ar - nr * ai) / den
    br = br_ref[...]
    bi = bi_ref[...]
    bbr = fre * br - fim * bi
    bbi = fre * bi + fim * br
    bbr_ref[...] = bbr.astype(BF16)
    bbi_ref[...] = bbi.astype(BF16)
    lbr_ref[...] = (lre * bbr - lim * bbi).astype(BF16)
    lbi_ref[...] = (lre * bbi + lim * bbr).astype(BF16)
    lam2r_ref[...] = lre * lre - lim * lim
    lam2i_ref[...] = 2.0 * (lre * lim)
    cro_ref[...] = cr_ref[...].astype(BF16)
    cio_ref[...] = (-ci_ref[...]).astype(BF16)


def _s5_prep(a_re, a_im, log_dt, b_re, b_im, c_re, c_im):
    rows = S5_GROUPS * S5_GROUP
    rep = lambda a: jnp.repeat(a, S5_GROUP, axis=0)
    ar = rep(a_re)
    ai = rep(a_im)
    ldt = rep(jnp.broadcast_to(log_dt[:, None], (S5_GROUPS, S5_STATE)))
    bt = lambda b: jnp.transpose(b, (0, 2, 1)).reshape(rows, S5_STATE)
    cc = lambda c: c.reshape(rows, S5_STATE)
    shp = (rows, S5_STATE)
    spec = pl.BlockSpec(shp, lambda: (0, 0))
    outs = pl.pallas_call(
        _s5_prep_kernel,
        in_specs=[spec] * 7,
        out_specs=[spec] * 8,
        out_shape=[jax.ShapeDtypeStruct(shp, F32)] * 2 + [jax.ShapeDtypeStruct(shp, BF16)] * 6,
        name="s5_prep",
    )(ar, ai, ldt, bt(b_re), bt(b_im), cc(c_re), cc(c_im))
    return outs


def _s5_layout(prep):
    lam2r, lam2i, bbr, bbi, lbr, lbi, cr, nci = prep
    g, h, p = S5_GROUPS, S5_GROUP, S5_STATE
    sg = S5_SLAB_GROUPS
    eye = jnp.eye(sg, dtype=jnp.bool_)

    def in_block(re, im):
        v = jnp.stack([re.reshape(S5_SLABS, sg, h, p), im.reshape(S5_SLABS, sg, h, p)], axis=3)
        v = jnp.where(eye[None, :, None, :, None, None], v[:, :, :, None, :, :], jnp.zeros((), v.dtype))
        v = v.reshape(S5_SLABS, sg, h, sg // 2, 2, 2, p)
        v = jnp.transpose(v, (0, 1, 2, 3, 5, 4, 6))
        return v.reshape(S5_SLABS, sg * h, S5_SLAB_COLS)

    wb = jnp.concatenate([in_block(bbr, bbi), in_block(lbr, lbi)], axis=1)

    og = 2 * sg
    eye_o = jnp.eye(og, dtype=jnp.bool_)
    c = jnp.stack([cr.reshape(S5_OCTETS, og, h, p), nci.reshape(S5_OCTETS, og, h, p)], axis=3)
    c = jnp.where(eye_o[None, :, :, None, None, None], c[:, None], jnp.zeros((), c.dtype))
    c = jnp.transpose(c, (0, 1, 4, 5, 2, 3))
    c = c.reshape(S5_OCTETS, 2, sg // 2, 2, 2, p, og * h)
    c = jnp.transpose(c, (0, 1, 2, 4, 3, 5, 6))
    wc = c.reshape(S5_OCTETS, S5_OCTET_COLS, og * h)

    def cols(v):
        v = v.reshape(g, h, p)[:, 0, :].reshape(S5_SLABS, sg // 2, 2, p)
        return v
    lam2 = jnp.stack([cols(lam2r), cols(lam2i)], axis=2)
    lam2 = lam2.reshape(1, S5_COLS)
    return wb, wc, lam2


def _s5_kernel(x_ref, mod_ref, g_ref, wb_ref, wc_ref, lam2_ref, dskip_ref, wglu_ref, o_ref,
               lhs_ref, lhs2_ref, w_ref, carry_ref, y_ref, mix_ref, *, tm, nb):
    rows = nb * tm
    step = pl.program_id(0)

    @pl.when(step == 0)
    def _():
        carry_ref[...] = jnp.zeros_like(carry_ref)
        lhs2_ref[...] = jnp.zeros_like(lhs2_ref)

    @pl.when(step > 0)
    def _():
        lhs2_ref[:, 0:SUBLANES, :] = lhs2_ref[:, rows:rows + SUBLANES, :]

    g = g_ref[...]
    for b in range(nb):
        sh = mod_ref[b, 0:1, :]
        sc = mod_ref[b, 1:2, :]
        h = _rms(x_ref[b], g) * (1.0 + sc) + sh
        for s in range(S5_SLABS):
            hs = h[:, s * LANES:(s + 1) * LANES]
            lhs_ref[s, pl.ds(b, tm, stride=nb), :] = hs
            lhs2_ref[s, pl.ds(nb + b, tm, stride=nb), :] = hs

    for n in range(S5_OCTETS):
        for sl in range(2):
            s = 2 * n + sl
            a = jnp.concatenate([lhs_ref[s], lhs2_ref[s, 0:rows, :]], axis=1).astype(BF16)
            w_ref[:, sl * S5_SLAB_COLS:(sl + 1) * S5_SLAB_COLS] = jnp.dot(
                a, wb_ref[s], preferred_element_type=F32)

        base = n * S5_OCTET_COLS
        lam = [jnp.broadcast_to(lam2_ref[:, base + j * LANES: base + (j + 1) * LANES], (SUBLANES, LANES))
               for j in range(2 * S5_TILES)]
        carry0 = tuple(carry_ref[:, base + j * LANES: base + (j + 1) * LANES] for j in range(2 * S5_TILES))

        def body(k, carry):
            r0 = pl.multiple_of(k * SUBLANES, SUBLANES)
            wk = w_ref[pl.ds(r0, SUBLANES), :]
            new = []
            for i in range(S5_TILES):
                lr, li = lam[2 * i], lam[2 * i + 1]
                yr, yi = carry[2 * i], carry[2 * i + 1]
                wr = wk[:, (2 * i) * LANES:(2 * i + 1) * LANES]
                wi = wk[:, (2 * i + 1) * LANES:(2 * i + 2) * LANES]
                new.append(lr * yr - li * yi + wr)
                new.append(lr * yi + li * yr + wi)
            w_ref[pl.ds(r0, SUBLANES), :] = jnp.concatenate(new, axis=1)
            return tuple(new)

        carry = lax.fori_loop(0, rows // SUBLANES, body, carry0)
        for j in range(2 * S5_TILES):
            carry_ref[:, base + j * LANES: base + (j + 1) * LANES] = carry[j]

        y_ref[:, n * 2 * LANES:(n + 1) * 2 * LANES] = jnp.dot(
            w_ref[...].astype(BF16), wc_ref[n], preferred_element_type=F32)

    u = jnp.concatenate([lhs_ref[s] for s in range(S5_SLABS)], axis=1)
    y = y_ref[...] + dskip_ref[...] * u
    z = jax.nn.gelu(y).astype(BF16)
    gl = jnp.dot(z, wglu_ref[...], preferred_element_type=F32)
    mix = gl[:, :D_MODEL] * jax.nn.sigmoid(gl[:, D_MODEL:])
    for s in range(S5_SLABS):
        mix_ref[s] = mix[:, s * LANES:(s + 1) * LANES]
    for b in range(nb):
        yb = jnp.concatenate([mix_ref[s, pl.ds(b, tm, stride=nb), :] for s in range(S5_SLABS)], axis=1)
        o_ref[b] = x_ref[b] + mod_ref[b, 2:3, :] * yb


def _s5_layer(x, mod, g_mix, wb, wc, lam2, d_skip, w_glu):
    nb, seq, d = x.shape
    tm = S5_TM
    rows = nb * tm
    kern = functools.partial(_s5_kernel, tm=tm, nb=nb)
    return pl.pallas_call(
        kern,
        grid=(seq // tm,),
        in_specs=[
            pl.BlockSpec((nb, tm, d), lambda i: (0, i, 0)),
            _const_spec((nb, 6, d)),
            _const_spec((1, d)),
            _const_spec(wb.shape),
            _const_spec(wc.shape),
            _const_spec(lam2.shape),
            _const_spec((1, d)),
            _const_spec(w_glu.shape),
        ],
        out_specs=pl.BlockSpec((nb, tm, d), lambda i: (0, i, 0)),
        out_shape=jax.ShapeDtypeStruct(x.shape, F32),
        scratch_shapes=[
            pltpu.VMEM((S5_SLABS, rows, LANES), F32),
            pltpu.VMEM((S5_SLABS, rows + SUBLANES, LANES), F32),
            pltpu.VMEM((rows, S5_OCTET_COLS), F32),
            pltpu.VMEM((SUBLANES, S5_COLS), F32),
            pltpu.VMEM((rows, d), F32),
            pltpu.VMEM((S5_SLABS, rows, LANES), F32),
        ],
        compiler_params=_cparams(("arbitrary",)),
        name="s5_layer",
    )(x, mod, g_mix.reshape(1, d), wb, wc, lam2, d_skip.reshape(1, d), w_glu)


def _mlp_kernel(x_ref, mod_ref, g_ref, w1_ref, w2_ref, gf_ref, o_ref, *, final):
    x = x_ref[...]
    sh = mod_ref[0, 3:4, :]
    sc = mod_ref[0, 4:5, :]
    gt = mod_ref[0, 5:6, :]
    h = (_rms(x, g_ref[...]) * (1.0 + sc) + sh).astype(BF16)
    a = jnp.dot(h, w1_ref[...], preferred_element_type=F32)
    a = jnp.maximum(a, 0.0)
    a = (a * a).astype(BF16)
    y = jnp.dot(a, w2_ref[...], preferred_element_type=F32)
    out = x + gt * y
    if final:
        out = _rms(out, gf_ref[...])
    o_ref[...] = out


def _mlp_layer(x, mod, g_mlp, w1, w2, g_final, final):
    nb, seq, d = x.shape
    tm = MLP_TM
    per_b = seq // tm
    x2 = x.reshape(nb * seq, d)
    out = pl.pallas_call(
        functools.partial(_mlp_kernel, final=final),
        grid=(nb * per_b,),
        in_specs=[
            pl.BlockSpec((tm, d), lambda i: (i, 0)),
            pl.BlockSpec((1, 6, d), lambda i: (i // per_b, 0, 0)),
            _const_spec((1, d)),
            _const_spec(w1.shape),
            _const_spec(w2.shape),
            _const_spec((1, d)),
        ],
        out_specs=pl.BlockSpec((tm, d), lambda i: (i, 0)),
        out_shape=jax.ShapeDtypeStruct(x2.shape, F32),
        compiler_params=_cparams(("arbitrary",)),
        name="mlp_final" if final else "mlp_layer",
    )(x2, mod, g_mlp.reshape(1, d), w1, w2, g_final.reshape(1, d))
    return out.reshape(nb, seq, d)


def _gla_kernel(x_ref, mod_ref, g_ref, win_ref, wg2_ref, bg_ref, gn_ref, wout_ref, o_ref,
                st_ref, obuf_ref, *, tm):
    @pl.when(pl.program_id(1) == 0)
    def _():
        st_ref[...] = jnp.zeros_like(st_ref)

    x = x_ref[0]
    sh = mod_ref[0, 0:1, :]
    sc = mod_ref[0, 1:2, :]
    gt = mod_ref[0, 2:3, :]
    h = (_rms(x, g_ref[...]) * (1.0 + sc) + sh).astype(BF16)
    proj = jnp.dot(h, win_ref[...], preferred_element_type=F32)
    o_q, o_k, o_v, o_r, o_g = 0, GLA_QK, 2 * GLA_QK, 2 * GLA_QK + D_MODEL, 2 * GLA_QK + 2 * D_MODEL
    q = (proj[:, o_q:o_k] * (GLA_DK ** -0.5)).astype(BF16)
    k = proj[:, o_k:o_v]
    v = proj[:, o_v:o_r].astype(BF16)
    r = proj[:, o_r:o_g]
    glr = proj[:, o_g:o_g + GLA_GLR_PAD].astype(BF16)
    pre = jnp.dot(glr, wg2_ref[...], preferred_element_type=F32) + bg_ref[...]
    la = (jnp.minimum(pre, 0.0) - jnp.log1p(jnp.exp(-jnp.abs(pre)))) * (1.0 / GLA_GATE_TAU)

    ti = lax.broadcasted_iota(jnp.int32, (CHUNK, CHUNK), 0)
    si = lax.broadcasted_iota(jnp.int32, (CHUNK, CHUNK), 1)
    upper = (si > ti).astype(F32)

    for c in range(tm // CHUNK):
        rs = slice(c * CHUNK, (c + 1) * CHUNK)
        la_c = la[rs]
        rc = jnp.dot(upper, la_c, preferred_element_type=F32, precision=lax.Precision.HIGHEST)
        gend = jnp.sum(la_c, axis=0, keepdims=True)
        kd = (k[rs] * jnp.exp(rc)).astype(BF16)
        dec = jnp.exp(gend)
        for hh in range(GLA_HEADS):
            ks = slice(hh * GLA_DK, (hh + 1) * GLA_DK)
            vs = slice(hh * GLA_DV, (hh + 1) * GLA_DV)
            kvt = lax.dot_general(v[rs, vs], kd[:, ks], (((0,), (0,)), ((), ())),
                                  preferred_element_type=F32)
            st = dec[:, ks] * st_ref[hh] + kvt
            st_ref[hh] = st
            obuf_ref[rs, vs] = lax.dot_general(q[rs, ks], st.astype(BF16), (((1,), (1,)), ((), ())),
                                               preferred_element_type=F32)

    outs = []
    for hh in range(GLA_HEADS):
        oh = obuf_ref[:, hh * GLA_DV:(hh + 1) * GLA_DV]
        outs.append(oh * lax.rsqrt(jnp.mean(oh * oh, axis=-1, keepdims=True) + EPS))
    o = jnp.concatenate(outs, axis=1) * gn_ref[...]
    o = (o * (r * jax.nn.sigmoid(r))).astype(BF16)
    y = jnp.dot(o, wout_ref[...], preferred_element_type=F32)
    o_ref[0] = x + gt * y


def _gla_layer(x, mod, g_mix, w_in_p, w_g2_p, b_gate, g_norm, w_out):
    nb, seq, d = x.shape
    tm = GLA_TM
    return pl.pallas_call(
        functools.partial(_gla_kernel, tm=tm),
        grid=(nb, seq // tm),
        in_specs=[
            pl.BlockSpec((1, tm, d), lambda b, i: (b, i, 0)),
            pl.BlockSpec((1, 6, d), lambda b, i: (b, 0, 0)),
            _const_spec((1, d)),
            _const_spec(w_in_p.shape),
            _const_spec(w_g2_p.shape),
            _const_spec((1, GLA_QK)),
            _const_spec((1, d)),
            _const_spec(w_out.shape),
        ],
        out_specs=pl.BlockSpec((1, tm, d), lambda b, i: (b, i, 0)),
        out_shape=jax.ShapeDtypeStruct(x.shape, F32),
        scratch_shapes=[
            pltpu.VMEM((GLA_HEADS, GLA_DV, GLA_DK), F32),
            pltpu.VMEM((tm, d), F32),
        ],
        compiler_params=_cparams(("arbitrary", "arbitrary")),
        name="gla_layer",
    )(x, mod, g_mix.reshape(1, d), w_in_p, w_g2_p, b_gate.reshape(1, GLA_QK), g_norm.reshape(1, d), w_out)


def kernel(x, c, w_ada, b_ada, norm_mix, norm_mlp, s5_a_re, s5_a_im, s5_log_dt, s5_b_re, s5_b_im,
           s5_c_re, s5_c_im, s5_d, s5_w_glu, gla_w_in, gla_w_gate2, gla_b_gate, gla_g_norm, gla_w_out,
           w_ff1, w_ff2, norm_final):
    bsz = x.shape[0]
    mod = _adaln(c, w_ada, b_ada).reshape(DEPTH, bsz, 6, D_MODEL)

    prep = _s5_prep(s5_a_re[0], s5_a_im[0], s5_log_dt[0], s5_b_re[0], s5_b_im[0], s5_c_re[0], s5_c_im[0])
    wb, wc, lam2 = _s5_layout(prep)
    x = _s5_layer(x, mod[0], norm_mix[0], wb, wc, lam2, s5_d[0], s5_w_glu[0].astype(BF16))
    x = _mlp_layer(x, mod[0], norm_mlp[0], w_ff1[0].astype(BF16), w_ff2[0].astype(BF16), norm_final, False)

    w_in = gla_w_in[0]
    o_v = 2 * GLA_QK
    o_g = o_v + D_MODEL
    o_r = o_g + GLA_GATE_RANK
    w_in_p = jnp.concatenate(
        [w_in[:, :o_g], w_in[:, o_r:], w_in[:, o_g:o_r],
         jnp.zeros((D_MODEL, GLA_GLR_PAD - GLA_GATE_RANK), w_in.dtype)], axis=1)
    w_in_p = w_in_p.astype(BF16)
    w_g2_p = jnp.concatenate(
        [gla_w_gate2[0], jnp.zeros((GLA_GLR_PAD - GLA_GATE_RANK, GLA_QK), gla_w_gate2.dtype)], axis=0).astype(BF16)
    x = _gla_layer(x, mod[1], norm_mix[1], w_in_p, w_g2_p, gla_b_gate[0], gla_g_norm[0],
                   gla_w_out[0].astype(BF16))
    x = _mlp_layer(x, mod[1], norm_mlp[1], w_ff1[1].astype(BF16), w_ff2[1].astype(BF16), norm_final, True)
    return x
```

```python
import functools
import math

import jax
import jax.numpy as jnp
from jax import lax
from jax.experimental import pallas as pl
from jax.experimental.pallas import tpu as pltpu

F32 = jnp.float32
BF16 = jnp.bfloat16

D_MODEL = 1024
DEPTH = 2
CHUNK = 64
S5_GROUP = 16
S5_GROUPS = D_MODEL // S5_GROUP
S5_STATE = 64
GLA_HEADS = 4
GLA_QK = D_MODEL // 2
GLA_DK = GLA_QK // GLA_HEADS
GLA_DV = D_MODEL // GLA_HEADS
GLA_GATE_RANK = 16
GLA_GATE_TAU = 16.0
D_FF = 4 * D_MODEL
EPS = 1e-6

LANES = 128
SUBLANES = 8
VMEM_LIMIT_BYTES = 56 * 1024 * 1024

S5_SLABS = D_MODEL // LANES
S5_SLAB_GROUPS = LANES // S5_GROUP
S5_SLAB_COLS = S5_SLAB_GROUPS * 2 * S5_STATE
S5_OCTETS = S5_SLABS // 2
S5_OCTET_COLS = 2 * S5_SLAB_COLS
S5_TILES = S5_OCTET_COLS // (2 * LANES)
S5_COLS = S5_SLABS * S5_SLAB_COLS

S5_TM = 128
MLP_TM = 512
GLA_TM = 128

GLA_GLR_PAD = LANES
GLA_IN_PAD = 2 * GLA_QK + 2 * D_MODEL + GLA_GLR_PAD


def _cparams(sem):
    return pltpu.CompilerParams(dimension_semantics=sem, vmem_limit_bytes=VMEM_LIMIT_BYTES)


def _const_spec(shape):
    nd = len(shape)
    return pl.BlockSpec(shape, lambda *_: (0,) * nd, pipeline_mode=pl.Buffered(1))


def _rms(x, g):
    return x * lax.rsqrt(jnp.mean(x * x, axis=-1, keepdims=True) + EPS) * g


def _adaln_kernel(c_ref, w_ref, b_ref, o_ref):
    c = c_ref[...]
    cs = (c * jax.nn.sigmoid(c)).astype(BF16)
    o_ref[0] = jnp.dot(cs, w_ref[0].astype(BF16), preferred_element_type=F32) + b_ref[0]


def _adaln(c, w_ada, b_ada):
    nb = c.shape[0]
    bsz = -(-nb // SUBLANES) * SUBLANES
    c = jnp.pad(c, ((0, bsz - nb), (0, 0)))
    n = w_ada.shape[-1]
    tn = 1536
    out = pl.pallas_call(
        _adaln_kernel,
        grid=(DEPTH, n // tn),
        in_specs=[
            pl.BlockSpec((bsz, D_MODEL), lambda i, j: (0, 0)),
            pl.BlockSpec((1, D_MODEL, tn), lambda i, j: (i, 0, j)),
            pl.BlockSpec((1, 1, tn), lambda i, j: (i, 0, j)),
        ],
        out_specs=pl.BlockSpec((1, bsz, tn), lambda i, j: (i, 0, j)),
        out_shape=jax.ShapeDtypeStruct((DEPTH, bsz, n), F32),
        compiler_params=_cparams(("arbitrary", "arbitrary")),
        name="adaln",
    )(c, w_ada, b_ada.reshape(DEPTH, 1, n))
    return out[:, :nb]


def _s5_prep_kernel(ar_ref, ai_ref, ldt_ref, br_ref, bi_ref, cr_ref, ci_ref,
                    lam2r_ref, lam2i_ref, bbr_ref, bbi_ref, lbr_ref, lbi_ref, cro_ref, cio_ref):
    dt = jnp.exp(ldt_ref[...])
    ar = ar_ref[...]
    ai = ai_ref[...]
    mag = jnp.exp(ar * dt)
    ph = ai * dt
    lre = mag * jnp.cos(ph)
    lim = mag * jnp.sin(ph)
    den = ar * ar + ai * ai
    nr = lre - 1.0
    ni = lim
    fre = (nr * ar + ni * ai) / den
    fim = (ni * ar - nr * ai) / den
    br = br_ref[...]
    bi = bi_ref[...]
    bbr = fre * br - fim * bi
    bbi = fre * bi + fim * br
    bbr_ref[...] = bbr.astype(BF16)
    bbi_ref[...] = bbi.astype(BF16)
    lbr_ref[...] = (lre * bbr - lim * bbi).astype(BF16)
    lbi_ref[...] = (lre * bbi + lim * bbr).astype(BF16)
    lam2r_ref[...] = lre * lre - lim * lim
    lam2i_ref[...] = 2.0 * (lre * lim)
    cro_ref[...] = cr_ref[...].astype(BF16)
    cio_ref[...] = (-ci_ref[...]).astype(BF16)


def _s5_prep(a_re, a_im, log_dt, b_re, b_im, c_re, c_im):
    rows = S5_GROUPS * S5_GROUP
    rep = lambda a: jnp.repeat(a, S5_GROUP, axis=0)
    ar = rep(a_re)
    ai = rep(a_im)
    ldt = rep(jnp.broadcast_to(log_dt[:, None], (S5_GROUPS, S5_STATE)))
    bt = lambda b: jnp.transpose(b, (0, 2, 1)).reshape(rows, S5_STATE)
    cc = lambda c: c.reshape(rows, S5_STATE)
    shp = (rows, S5_STATE)
    spec = pl.BlockSpec(shp, lambda: (0, 0))
    outs = pl.pallas_call(
        _s5_prep_kernel,
        in_specs=[spec] * 7,
        out_specs=[spec] * 8,
        out_shape=[jax.ShapeDtypeStruct(shp, F32)] * 2 + [jax.ShapeDtypeStruct(shp, BF16)] * 6,
        name="s5_prep",
    )(ar, ai, ldt, bt(b_re), bt(b_im), cc(c_re), cc(c_im))
    return outs


def _s5_layout(prep):
    lam2r, lam2i, bbr, bbi, lbr, lbi, cr, nci = prep
    g, h, p = S5_GROUPS, S5_GROUP, S5_STATE
    sg = S5_SLAB_GROUPS
    eye = jnp.eye(sg, dtype=jnp.bool_)

    def in_block(re, im):
        v = jnp.stack([re.reshape(S5_SLABS, sg, h, p), im.reshape(S5_SLABS, sg, h, p)], axis=3)
        v = jnp.where(eye[None, :, None, :, None, None], v[:, :, :, None, :, :], jnp.zeros((), v.dtype))
        v = v.reshape(S5_SLABS, sg, h, sg // 2, 2, 2, p)
        v = jnp.transpose(v, (0, 1, 2, 3, 5, 4, 6))
        return v.reshape(S5_SLABS, sg * h, S5_SLAB_COLS)

    wb = jnp.concatenate([in_block(bbr, bbi), in_block(lbr, lbi)], axis=1)

    og = 2 * sg
    eye_o = jnp.eye(og, dtype=jnp.bool_)
    c = jnp.stack([cr.reshape(S5_OCTETS, og, h, p), nci.reshape(S5_OCTETS, og, h, p)], axis=3)
    c = jnp.where(eye_o[None, :, :, None, None, None], c[:, None], jnp.zeros((), c.dtype))
    c = jnp.transpose(c, (0, 1, 4, 5, 2, 3))
    c = c.reshape(S5_OCTETS, 2, sg // 2, 2, 2, p, og * h)
    c = jnp.transpose(c, (0, 1, 2, 4, 3, 5, 6))
    wc = c.reshape(S5_OCTETS, S5_OCTET_COLS, og * h)

    def cols(v):
        v = v.reshape(g, h, p)[:, 0, :].reshape(S5_SLABS, sg // 2, 2, p)
        return v
    lam2 = jnp.stack([cols(lam2r), cols(lam2i)], axis=2)
    lam2 = jnp.broadcast_to(lam2.reshape(1, S5_COLS), (SUBLANES, S5_COLS))
    return wb, wc, lam2


def _s5_kernel(x_ref, mod_ref, g_ref, wb_ref, wc_ref, lam2_ref, dskip_ref, wglu_ref, o_ref,
               lhs_ref, lhs2_ref, w_ref, carry_ref, y_ref, mix_ref, *, tm, nb):
    rows = nb * tm
    step = pl.program_id(0)

    @pl.when(step == 0)
    def _():
        carry_ref[...] = jnp.zeros_like(carry_ref)
        lhs2_ref[...] = jnp.zeros_like(lhs2_ref)

    @pl.when(step > 0)
    def _():
        lhs2_ref[:, 0:SUBLANES, :] = lhs2_ref[:, rows:rows + SUBLANES, :]

    g = g_ref[...]
    for b in range(nb):
        sh = mod_ref[b, 0:1, :]
        sc = mod_ref[b, 1:2, :]
        h = _rms(x_ref[b], g) * (1.0 + sc) + sh
        for s in range(S5_SLABS):
            hs = h[:, s * LANES:(s + 1) * LANES]
            lhs_ref[s, pl.ds(b, tm, stride=nb), :] = hs
            lhs2_ref[s, pl.ds(nb + b, tm, stride=nb), :] = hs

    for n in range(S5_OCTETS):
        slot = n % 2
        for sl in range(2):
            s = 2 * n + sl
            a = jnp.concatenate([lhs_ref[s], lhs2_ref[s, 0:rows, :]], axis=1).astype(BF16)
            w_ref[slot, :, sl * S5_SLAB_COLS:(sl + 1) * S5_SLAB_COLS] = jnp.dot(
                a, wb_ref[s], preferred_element_type=F32)

        base = n * S5_OCTET_COLS
        for i in range(S5_TILES):
            cr = slice(base + (2 * i) * LANES, base + (2 * i + 1) * LANES)
            ci = slice(base + (2 * i + 1) * LANES, base + (2 * i + 2) * LANES)
            lr, li = lam2_ref[:, cr], lam2_ref[:, ci]
            yr, yi = carry_ref[:, cr], carry_ref[:, ci]
            wr_c = slice((2 * i) * LANES, (2 * i + 1) * LANES)
            wi_c = slice((2 * i + 1) * LANES, (2 * i + 2) * LANES)
            for k in range(rows // SUBLANES):
                rk = slice(k * SUBLANES, (k + 1) * SUBLANES)
                nr = lr * yr - li * yi + w_ref[slot, rk, wr_c]
                ni = lr * yi + li * yr + w_ref[slot, rk, wi_c]
                w_ref[slot, rk, wr_c] = nr
                w_ref[slot, rk, wi_c] = ni
                yr, yi = nr, ni
            carry_ref[:, cr] = yr
            carry_ref[:, ci] = yi

        y_ref[:, n * 2 * LANES:(n + 1) * 2 * LANES] = jnp.dot(
            w_ref[slot].astype(BF16), wc_ref[n], preferred_element_type=F32)

    u = jnp.concatenate([lhs_ref[s] for s in range(S5_SLABS)], axis=1)
    y = y_ref[...] + dskip_ref[...] * u
    z = jax.nn.gelu(y).astype(BF16)
    gl = jnp.dot(z, wglu_ref[...], preferred_element_type=F32)
    mix = gl[:, :D_MODEL] * jax.nn.sigmoid(gl[:, D_MODEL:])
    for s in range(S5_SLABS):
        mix_ref[s] = mix[:, s * LANES:(s + 1) * LANES]
    for b in range(nb):
        yb = jnp.concatenate([mix_ref[s, pl.ds(b, tm, stride=nb), :] for s in range(S5_SLABS)], axis=1)
        o_ref[b] = x_ref[b] + mod_ref[b, 2:3, :] * yb


def _s5_layer(x, mod, g_mix, wb, wc, lam2, d_skip, w_glu):
    nb, seq, d = x.shape
    tm = S5_TM
    rows = nb * tm
    kern = functools.partial(_s5_kernel, tm=tm, nb=nb)
    return pl.pallas_call(
        kern,
        grid=(seq // tm,),
        in_specs=[
            pl.BlockSpec((nb, tm, d), lambda i: (0, i, 0)),
            _const_spec((nb, 6, d)),
            _const_spec((1, d)),
            _const_spec(wb.shape),
            _const_spec(wc.shape),
            _const_spec(lam2.shape),
            _const_spec((1, d)),
            _const_spec(w_glu.shape),
        ],
        out_specs=pl.BlockSpec((nb, tm, d), lambda i: (0, i, 0)),
        out_shape=jax.ShapeDtypeStruct(x.shape, F32),
        scratch_shapes=[
            pltpu.VMEM((S5_SLABS, rows, LANES), F32),
            pltpu.VMEM((S5_SLABS, rows + SUBLANES, LANES), F32),
            pltpu.VMEM((2, rows, S5_OCTET_COLS), F32),
            pltpu.VMEM((SUBLANES, S5_COLS), F32),
            pltpu.VMEM((rows, d), F32),
            pltpu.VMEM((S5_SLABS, rows, LANES), F32),
        ],
        compiler_params=_cparams(("arbitrary",)),
        name="s5_layer",
    )(x, mod, g_mix.reshape(1, d), wb, wc, lam2, d_skip.reshape(1, d), w_glu)


def _mlp_kernel(x_ref, mod_ref, g_ref, w1_ref, w2_ref, gf_ref, o_ref, *, final):
    x = x_ref[...]
    sh = mod_ref[0, 3:4, :]
    sc = mod_ref[0, 4:5, :]
    gt = mod_ref[0, 5:6, :]
    h = (_rms(x, g_ref[...]) * (1.0 + sc) + sh).astype(BF16)
    a = jnp.dot(h, w1_ref[...], preferred_element_type=F32)
    a = jnp.maximum(a, 0.0)
    a = (a * a).astype(BF16)
    y = jnp.dot(a, w2_ref[...], preferred_element_type=F32)
    out = x + gt * y
    if final:
        out = _rms(out, gf_ref[...])
    o_ref[...] = out


def _mlp_layer(x, mod, g_mlp, w1, w2, g_final, final):
    nb, seq, d = x.shape
    tm = MLP_TM
    per_b = seq // tm
    x2 = x.reshape(nb * seq, d)
    out = pl.pallas_call(
        functools.partial(_mlp_kernel, final=final),
        grid=(nb * per_b,),
        in_specs=[
            pl.BlockSpec((tm, d), lambda i: (i, 0)),
            pl.BlockSpec((1, 6, d), lambda i: (i // per_b, 0, 0)),
            _const_spec((1, d)),
            _const_spec(w1.shape),
            _const_spec(w2.shape),
            _const_spec((1, d)),
        ],
        out_specs=pl.BlockSpec((tm, d), lambda i: (i, 0)),
        out_shape=jax.ShapeDtypeStruct(x2.shape, F32),
        compiler_params=_cparams(("arbitrary",)),
        name="mlp_final" if final else "mlp_layer",
    )(x2, mod, g_mlp.reshape(1, d), w1, w2, g_final.reshape(1, d))
    return out.reshape(nb, seq, d)


def _gla_kernel(x_ref, mod_ref, g_ref, win_ref, wg2_ref, bg_ref, gn_ref, wout_ref, o_ref,
                st_ref, obuf_ref, *, tm, nb):
    @pl.when(pl.program_id(0) == 0)
    def _():
        st_ref[...] = jnp.zeros_like(st_ref)

    g = g_ref[...]
    h = jnp.concatenate(
        [(_rms(x_ref[b], g) * (1.0 + mod_ref[b, 1:2, :]) + mod_ref[b, 0:1, :]).astype(BF16) for b in range(nb)],
        axis=0)
    proj = jnp.dot(h, win_ref[...], preferred_element_type=F32)
    o_q, o_k, o_v, o_r, o_g = 0, GLA_QK, 2 * GLA_QK, 2 * GLA_QK + D_MODEL, 2 * GLA_QK + 2 * D_MODEL
    q = (proj[:, o_q:o_k] * (GLA_DK ** -0.5)).astype(BF16)
    k = proj[:, o_k:o_v]
    v = proj[:, o_v:o_r].astype(BF16)
    r = proj[:, o_r:o_g]
    glr = proj[:, o_g:o_g + GLA_GLR_PAD].astype(BF16)
    pre = jnp.dot(glr, wg2_ref[...], preferred_element_type=F32) + bg_ref[...]
    la = (jnp.minimum(pre, 0.0) - jnp.log(1.0 + jnp.exp(-jnp.abs(pre)))) * (1.0 / GLA_GATE_TAU)
    la_hi = la.astype(BF16)
    la_lo = (la - la_hi.astype(F32)).astype(BF16)
    la2 = jnp.concatenate([la_hi, la_lo], axis=1)

    ti = lax.broadcasted_iota(jnp.int32, (CHUNK, CHUNK), 0)
    si = lax.broadcasted_iota(jnp.int32, (CHUNK, CHUNK), 1)
    upper = (si > ti).astype(BF16)

    for b in range(nb):
        for c in range(tm // CHUNK):
            rs = slice(b * tm + c * CHUNK, b * tm + (c + 1) * CHUNK)
            rc2 = jnp.dot(upper, la2[rs], preferred_element_type=F32)
            rc = rc2[:, :GLA_QK] + rc2[:, GLA_QK:]
            gend = jnp.sum(la[rs], axis=0, keepdims=True)
            kd = (k[rs] * jnp.exp(rc)).astype(BF16)
            dec = jnp.exp(gend)
            for hh in range(GLA_HEADS):
                ks = slice(hh * GLA_DK, (hh + 1) * GLA_DK)
                vs = slice(hh * GLA_DV, (hh + 1) * GLA_DV)
                kvt = lax.dot_general(v[rs, vs], kd[:, ks], (((0,), (0,)), ((), ())),
                                      preferred_element_type=F32)
                st = dec[:, ks] * st_ref[b, hh] + kvt
                st_ref[b, hh] = st
                obuf_ref[rs, vs] = lax.dot_general(q[rs, ks], st.astype(BF16), (((1,), (1,)), ((), ())),
                                                   preferred_element_type=F32)

    outs = []
    for hh in range(GLA_HEADS):
        oh = obuf_ref[:, hh * GLA_DV:(hh + 1) * GLA_DV]
        outs.append(oh * lax.rsqrt(jnp.mean(oh * oh, axis=-1, keepdims=True) + EPS))
    o = jnp.concatenate(outs, axis=1) * gn_ref[...]
    o = (o * (r * jax.nn.sigmoid(r))).astype(BF16)
    y = jnp.dot(o, wout_ref[...], preferred_element_type=F32)
    for b in range(nb):
        o_ref[b] = x_ref[b] + mod_ref[b, 2:3, :] * y[b * tm:(b + 1) * tm]


def _gla_layer(x, mod, g_mix, w_in_p, w_g2_p, b_gate, g_norm, w_out):
    nb, seq, d = x.shape
    tm = GLA_TM
    return pl.pallas_call(
        functools.partial(_gla_kernel, tm=tm, nb=nb),
        grid=(seq // tm,),
        in_specs=[
            pl.BlockSpec((nb, tm, d), lambda i: (0, i, 0)),
            _const_spec((nb, 6, d)),
            _const_spec((1, d)),
            _const_spec(w_in_p.shape),
            _const_spec(w_g2_p.shape),
            _const_spec((1, GLA_QK)),
            _const_spec((1, d)),
            _const_spec(w_out.shape),
        ],
        out_specs=pl.BlockSpec((nb, tm, d), lambda i: (0, i, 0)),
        out_shape=jax.ShapeDtypeStruct(x.shape, F32),
        scratch_shapes=[
            pltpu.VMEM((nb, GLA_HEADS, GLA_DV, GLA_DK), F32),
            pltpu.VMEM((nb * tm, d), F32),
        ],
        compiler_params=_cparams(("arbitrary",)),
        name="gla_layer",
    )(x, mod, g_mix.reshape(1, d), w_in_p, w_g2_p, b_gate.reshape(1, GLA_QK), g_norm.reshape(1, d), w_out)


def kernel(x, c, w_ada, b_ada, norm_mix, norm_mlp, s5_a_re, s5_a_im, s5_log_dt, s5_b_re, s5_b_im,
           s5_c_re, s5_c_im, s5_d, s5_w_glu, gla_w_in, gla_w_gate2, gla_b_gate, gla_g_norm, gla_w_out,
           w_ff1, w_ff2, norm_final):
    bsz = x.shape[0]
    mod = _adaln(c, w_ada, b_ada).reshape(DEPTH, bsz, 6, D_MODEL)

    prep = _s5_prep(s5_a_re[0], s5_a_im[0], s5_log_dt[0], s5_b_re[0], s5_b_im[0], s5_c_re[0], s5_c_im[0])
    wb, wc, lam2 = _s5_layout(prep)
    x = _s5_layer(x, mod[0], norm_mix[0], wb, wc, lam2, s5_d[0], s5_w_glu[0].astype(BF16))
    x = _mlp_layer(x, mod[0], norm_mlp[0], w_ff1[0].astype(BF16), w_ff2[0].astype(BF16), norm_final, False)

    w_in = gla_w_in[0]
    o_v = 2 * GLA_QK
    o_g = o_v + D_MODEL
    o_r = o_g + GLA_GATE_RANK
    w_in_p = jnp.concatenate(
        [w_in[:, :o_g], w_in[:, o_r:], w_in[:, o_g:o_r],
         jnp.zeros((D_MODEL, GLA_GLR_PAD - GLA_GATE_RANK), w_in.dtype)], axis=1)
    w_in_p = w_in_p.astype(BF16)
    w_g2_p = jnp.concatenate(
        [gla_w_gate2[0], jnp.zeros((GLA_GLR_PAD - GLA_GATE_RANK, GLA_QK), gla_w_gate2.dtype)], axis=0).astype(BF16)
    x = _gla_layer(x, mod[1], norm_mix[1], w_in_p, w_g2_p, gla_b_gate[0], gla_g_norm[0],
                   gla_w_out[0].astype(BF16))
    x = _mlp_layer(x, mod[1], norm_mlp[1], w_ff1[1].astype(BF16), w_ff2[1].astype(BF16), norm_final, True)
    return x
```

```python
import functools

import jax
import jax.numpy as jnp
from jax import lax
from jax.experimental import pallas as pl
from jax.experimental.pallas import tpu as pltpu

F32 = jnp.float32
BF16 = jnp.bfloat16

D_MODEL = 1024
DEPTH = 2
CHUNK = 64
S5_GROUP = 16
S5_GROUPS = D_MODEL // S5_GROUP
S5_STATE = 64
GLA_HEADS = 4
GLA_QK = D_MODEL // 2
GLA_DK = GLA_QK // GLA_HEADS
GLA_DV = D_MODEL // GLA_HEADS
GLA_GATE_RANK = 16
GLA_GATE_TAU = 16.0
EPS = 1e-6

LANES = 128
SUBLANES = 8
VMEM_LIMIT_BYTES = 56 * 1024 * 1024

S5_SLABS = D_MODEL // LANES
S5_SLAB_GROUPS = LANES // S5_GROUP
S5_SLAB_COLS = S5_SLAB_GROUPS * 2 * S5_STATE
S5_OCTETS = S5_SLABS // 2
S5_OCTET_COLS = 2 * S5_SLAB_COLS
S5_TILES = S5_OCTET_COLS // (2 * LANES)
S5_COLS = S5_SLABS * S5_SLAB_COLS

S5_TM = 128
MLP_TM = 512
GLA_TM = 128

GLA_GLR_PAD = LANES


def _cparams(sem):
    return pltpu.CompilerParams(dimension_semantics=sem, vmem_limit_bytes=VMEM_LIMIT_BYTES)


def _const_spec(shape):
    nd = len(shape)
    return pl.BlockSpec(shape, lambda *_: (0,) * nd, pipeline_mode=pl.Buffered(1))


def _rms(x, g):
    return x * lax.rsqrt(jnp.mean(x * x, axis=-1, keepdims=True) + EPS) * g


def _adaln_kernel(c_ref, w_ref, b_ref, o_ref):
    c = c_ref[...]
    cs = (c * jax.nn.sigmoid(c)).astype(BF16)
    o_ref[0] = jnp.dot(cs, w_ref[0].astype(BF16), preferred_element_type=F32) + b_ref[0]


def _adaln(c, w_ada, b_ada):
    nb = c.shape[0]
    bsz = -(-nb // SUBLANES) * SUBLANES
    c = jnp.pad(c, ((0, bsz - nb), (0, 0)))
    n = w_ada.shape[-1]
    tn = 1536
    out = pl.pallas_call(
        _adaln_kernel,
        grid=(DEPTH, n // tn),
        in_specs=[
            pl.BlockSpec((bsz, D_MODEL), lambda i, j: (0, 0)),
            pl.BlockSpec((1, D_MODEL, tn), lambda i, j: (i, 0, j)),
            pl.BlockSpec((1, 1, tn), lambda i, j: (i, 0, j)),
        ],
        out_specs=pl.BlockSpec((1, bsz, tn), lambda i, j: (i, 0, j)),
        out_shape=jax.ShapeDtypeStruct((DEPTH, bsz, n), F32),
        compiler_params=_cparams(("arbitrary", "arbitrary")),
        name="adaln",
    )(c, w_ada, b_ada.reshape(DEPTH, 1, n))
    return out[:, :nb]


def _s5_prep_kernel(ar_ref, ai_ref, ldt_ref, br_ref, bi_ref, cr_ref, ci_ref,
                    lam2r_ref, lam2i_ref, bbr_ref, bbi_ref, lbr_ref, lbi_ref, cro_ref, cio_ref):
    dt = jnp.exp(ldt_ref[...])
    ar = ar_ref[...]
    ai = ai_ref[...]
    mag = jnp.exp(ar * dt)
    ph = ai * dt
    lre = mag * jnp.cos(ph)
    lim = mag * jnp.sin(ph)
    den = ar * ar + ai * ai
    nr = lre - 1.0
    ni = lim
    fre = (nr * ar + ni * ai) / den
    fim = (ni * ar - nr * ai) / den
    br = br_ref[...]
    bi = bi_ref[...]
    bbr = fre * br - fim * bi
    bbi = fre * bi + fim * br
    bbr_ref[...] = bbr.astype(BF16)
    bbi_ref[...] = bbi.astype(BF16)
    lbr_ref[...] = (lre * bbr - lim * bbi).astype(BF16)
    lbi_ref[...] = (lre * bbi + lim * bbr).astype(BF16)
    lam2r_ref[...] = lre * lre - lim * lim
    lam2i_ref[...] = 2.0 * (lre * lim)
    cro_ref[...] = cr_ref[...].astype(BF16)
    cio_ref[...] = (-ci_ref[...]).astype(BF16)


def _s5_prep(a_re, a_im, log_dt, b_re, b_im, c_re, c_im):
    rows = S5_GROUPS * S5_GROUP
    rep = lambda a: jnp.repeat(a, S5_GROUP, axis=0)
    ar = rep(a_re)
    ai = rep(a_im)
    ldt = rep(jnp.broadcast_to(log_dt[:, None], (S5_GROUPS, S5_STATE)))
    bt = lambda b: jnp.transpose(b, (0, 2, 1)).reshape(rows, S5_STATE)
    cc = lambda c: c.reshape(rows, S5_STATE)
    shp = (rows, S5_STATE)
    spec = pl.BlockSpec(shp, lambda: (0, 0))
    outs = pl.pallas_call(
        _s5_prep_kernel,
        in_specs=[spec] * 7,
        out_specs=[spec] * 8,
        out_shape=[jax.ShapeDtypeStruct(shp, F32)] * 2 + [jax.ShapeDtypeStruct(shp, BF16)] * 6,
        name="s5_prep",
    )(ar, ai, ldt, bt(b_re), bt(b_im), cc(c_re), cc(c_im))
    return outs


def _s5_layout(prep):
    lam2r, lam2i, bbr, bbi, lbr, lbi, cr, nci = prep
    g, h, p = S5_GROUPS, S5_GROUP, S5_STATE
    sg = S5_SLAB_GROUPS
    eye = jnp.eye(sg, dtype=jnp.bool_)

    def in_block(re, im):
        v = jnp.stack([re.reshape(S5_SLABS, sg, h, p), im.reshape(S5_SLABS, sg, h, p)], axis=3)
        v = jnp.where(eye[None, :, None, :, None, None], v[:, :, :, None, :, :], jnp.zeros((), v.dtype))
        v = v.reshape(S5_SLABS, sg, h, sg // 2, 2, 2, p)
        v = jnp.transpose(v, (0, 1, 2, 3, 5, 4, 6))
        return v.reshape(S5_SLABS, sg * h, S5_SLAB_COLS)

    wb = jnp.concatenate([in_block(bbr, bbi), in_block(lbr, lbi)], axis=1)

    og = 2 * sg
    eye_o = jnp.eye(og, dtype=jnp.bool_)
    c = jnp.stack([cr.reshape(S5_OCTETS, og, h, p), nci.reshape(S5_OCTETS, og, h, p)], axis=3)
    c = jnp.where(eye_o[None, :, :, None, None, None], c[:, None], jnp.zeros((), c.dtype))
    c = jnp.transpose(c, (0, 1, 4, 5, 2, 3))
    c = c.reshape(S5_OCTETS, 2, sg // 2, 2, 2, p, og * h)
    c = jnp.transpose(c, (0, 1, 2, 4, 3, 5, 6))
    wc = c.reshape(S5_OCTETS, S5_OCTET_COLS, og * h)

    def cols(v):
        return v.reshape(g, h, p)[:, 0, :].reshape(S5_SLABS, sg // 2, 2, p)
    lam2 = jnp.stack([cols(lam2r), cols(lam2i)], axis=2)
    lam2 = jnp.broadcast_to(lam2.reshape(1, S5_COLS), (SUBLANES, S5_COLS))
    return wb, wc, lam2


def _s5_kernel(x_ref, mod_ref, g_ref, wb_ref, wc_ref, lam2_ref, dskip_ref, wglu_ref, o_ref,
               lhs_ref, lhs2_ref, w_ref, carry_ref, y_ref, mix_ref, *, tm, nb):
    rows = nb * tm
    step = pl.program_id(0)

    @pl.when(step == 0)
    def _():
        carry_ref[...] = jnp.zeros_like(carry_ref)
        lhs2_ref[...] = jnp.zeros_like(lhs2_ref)

    @pl.when(step > 0)
    def _():
        lhs2_ref[:, 0:SUBLANES, :] = lhs2_ref[:, rows:rows + SUBLANES, :]

    g = g_ref[...]
    for b in range(nb):
        sh = mod_ref[b, 0:1, :]
        sc = mod_ref[b, 1:2, :]
        h = _rms(x_ref[b], g) * (1.0 + sc) + sh
        for s in range(S5_SLABS):
            hs = h[:, s * LANES:(s + 1) * LANES]
            lhs_ref[s, pl.ds(b, tm, stride=nb), :] = hs
            lhs2_ref[s, pl.ds(nb + b, tm, stride=nb), :] = hs

    def input_taps(n):
        for sl in range(2):
            s = 2 * n + sl
            a = jnp.concatenate([lhs_ref[s], lhs2_ref[s, 0:rows, :]], axis=1).astype(BF16)
            w_ref[n % 2, :, sl * S5_SLAB_COLS:(sl + 1) * S5_SLAB_COLS] = jnp.dot(
                a, wb_ref[s], preferred_element_type=F32)

    input_taps(0)
    for n in range(S5_OCTETS):
        slot = n % 2
        if n + 1 < S5_OCTETS:
            input_taps(n + 1)

        base = n * S5_OCTET_COLS
        for i in range(S5_TILES):
            cr = slice(base + (2 * i) * LANES, base + (2 * i + 1) * LANES)
            ci = slice(base + (2 * i + 1) * LANES, base + (2 * i + 2) * LANES)
            lr, li = lam2_ref[:, cr], lam2_ref[:, ci]
            yr, yi = carry_ref[:, cr], carry_ref[:, ci]
            wr_c = slice((2 * i) * LANES, (2 * i + 1) * LANES)
            wi_c = slice((2 * i + 1) * LANES, (2 * i + 2) * LANES)
            for k in range(rows // SUBLANES):
                rk = slice(k * SUBLANES, (k + 1) * SUBLANES)
                nr = lr * yr - li * yi + w_ref[slot, rk, wr_c]
                ni = lr * yi + li * yr + w_ref[slot, rk, wi_c]
                w_ref[slot, rk, wr_c] = nr
                w_ref[slot, rk, wi_c] = ni
                yr, yi = nr, ni
            carry_ref[:, cr] = yr
            carry_ref[:, ci] = yi

        y_ref[:, n * 2 * LANES:(n + 1) * 2 * LANES] = jnp.dot(
            w_ref[slot].astype(BF16), wc_ref[n], preferred_element_type=F32)

    u = jnp.concatenate([lhs_ref[s] for s in range(S5_SLABS)], axis=1)
    y = y_ref[...] + dskip_ref[...] * u
    z = jax.nn.gelu(y).astype(BF16)
    gl = jnp.dot(z, wglu_ref[...], preferred_element_type=F32)
    mix = gl[:, :D_MODEL] * jax.nn.sigmoid(gl[:, D_MODEL:])
    for s in range(S5_SLABS):
        mix_ref[s] = mix[:, s * LANES:(s + 1) * LANES]
    for b in range(nb):
        yb = jnp.concatenate([mix_ref[s, pl.ds(b, tm, stride=nb), :] for s in range(S5_SLABS)], axis=1)
        o_ref[b] = x_ref[b] + mod_ref[b, 2:3, :] * yb


def _s5_layer(x, mod, g_mix, wb, wc, lam2, d_skip, w_glu):
    nb, seq, d = x.shape
    tm = S5_TM
    rows = nb * tm
    kern = functools.partial(_s5_kernel, tm=tm, nb=nb)
    return pl.pallas_call(
        kern,
        grid=(seq // tm,),
        in_specs=[
            pl.BlockSpec((nb, tm, d), lambda i: (0, i, 0)),
            _const_spec((nb, 6, d)),
            _const_spec((1, d)),
            _const_spec(wb.shape),
            _const_spec(wc.shape),
            _const_spec(lam2.shape),
            _const_spec((1, d)),
            _const_spec(w_glu.shape),
        ],
        out_specs=pl.BlockSpec((nb, tm, d), lambda i: (0, i, 0)),
        out_shape=jax.ShapeDtypeStruct(x.shape, F32),
        scratch_shapes=[
            pltpu.VMEM((S5_SLABS, rows, LANES), F32),
            pltpu.VMEM((S5_SLABS, rows + SUBLANES, LANES), F32),
            pltpu.VMEM((2, rows, S5_OCTET_COLS), F32),
            pltpu.VMEM((SUBLANES, S5_COLS), F32),
            pltpu.VMEM((rows, d), F32),
            pltpu.VMEM((S5_SLABS, rows, LANES), F32),
        ],
        compiler_params=_cparams(("arbitrary",)),
        name="s5_layer",
    )(x, mod, g_mix.reshape(1, d), wb, wc, lam2, d_skip.reshape(1, d), w_glu)


def _mlp_kernel(x_ref, mod_ref, g_ref, w1_ref, w2_ref, gf_ref, o_ref, *, final):
    x = x_ref[...]
    sh = mod_ref[0, 3:4, :]
    sc = mod_ref[0, 4:5, :]
    gt = mod_ref[0, 5:6, :]
    h = (_rms(x, g_ref[...]) * (1.0 + sc) + sh).astype(BF16)
    a = jnp.dot(h, w1_ref[...], preferred_element_type=F32)
    a = jnp.maximum(a, 0.0)
    a = (a * a).astype(BF16)
    y = jnp.dot(a, w2_ref[...], preferred_element_type=F32)
    out = x + gt * y
    if final:
        out = _rms(out, gf_ref[...])
    o_ref[...] = out


def _mlp_layer(x, mod, g_mlp, w1, w2, g_final, final):
    nb, seq, d = x.shape
    tm = MLP_TM
    per_b = seq // tm
    x2 = x.reshape(nb * seq, d)
    out = pl.pallas_call(
        functools.partial(_mlp_kernel, final=final),
        grid=(nb * per_b,),
        in_specs=[
            pl.BlockSpec((tm, d), lambda i: (i, 0)),
            pl.BlockSpec((1, 6, d), lambda i: (i // per_b, 0, 0)),
            _const_spec((1, d)),
            _const_spec(w1.shape),
            _const_spec(w2.shape),
            _const_spec((1, d)),
        ],
        out_specs=pl.BlockSpec((tm, d), lambda i: (i, 0)),
        out_shape=jax.ShapeDtypeStruct(x2.shape, F32),
        compiler_params=_cparams(("arbitrary",)),
        name="mlp_final" if final else "mlp_layer",
    )(x2, mod, g_mlp.reshape(1, d), w1, w2, g_final.reshape(1, d))
    return out.reshape(nb, seq, d)


def _gla_kernel(x_ref, mod_ref, g_ref, win_ref, wg2_ref, bg_ref, gn_ref, wout_ref, o_ref,
                st_ref, obuf_ref, *, tm, nb):
    @pl.when(pl.program_id(0) == 0)
    def _():
        st_ref[...] = jnp.zeros_like(st_ref)

    g = g_ref[...]
    h = jnp.concatenate(
        [(_rms(x_ref[b], g) * (1.0 + mod_ref[b, 1:2, :]) + mod_ref[b, 0:1, :]).astype(BF16) for b in range(nb)],
        axis=0)
    proj = jnp.dot(h, win_ref[...], preferred_element_type=F32)
    o_q, o_k, o_v, o_r, o_g = 0, GLA_QK, 2 * GLA_QK, 2 * GLA_QK + D_MODEL, 2 * GLA_QK + 2 * D_MODEL
    q = (proj[:, o_q:o_k] * (GLA_DK ** -0.5)).astype(BF16)
    k = proj[:, o_k:o_v]
    v = proj[:, o_v:o_r].astype(BF16)
    r = proj[:, o_r:o_g]
    glr = proj[:, o_g:o_g + GLA_GLR_PAD].astype(BF16)
    pre = jnp.dot(glr, wg2_ref[...], preferred_element_type=F32) + bg_ref[...]
    la = (jnp.minimum(pre, 0.0) - jnp.log(1.0 + jnp.exp(-jnp.abs(pre)))) * (1.0 / GLA_GATE_TAU)
    la_hi = la.astype(BF16)
    la_lo = (la - la_hi.astype(F32)).astype(BF16)
    la2 = jnp.concatenate([la_hi, la_lo], axis=1)

    ti = lax.broadcasted_iota(jnp.int32, (CHUNK, CHUNK), 0)
    si = lax.broadcasted_iota(jnp.int32, (CHUNK, CHUNK), 1)
    upper = (si > ti).astype(BF16)

    nchunk = tm // CHUNK
    rows_of = lambda b, c: slice(b * tm + c * CHUNK, b * tm + (c + 1) * CHUNK)
    ks = lambda hh: slice(hh * GLA_DK, (hh + 1) * GLA_DK)
    vs = lambda hh: slice(hh * GLA_DV, (hh + 1) * GLA_DV)
    kd, dec = {}, {}
    for c in range(nchunk):
        for b in range(nb):
            rs = rows_of(b, c)
            rc2 = jnp.dot(upper, la2[rs], preferred_element_type=F32)
            rc = rc2[:, :GLA_QK] + rc2[:, GLA_QK:]
            kd[b, c] = (k[rs] * jnp.exp(rc)).astype(BF16)
            dec[b, c] = jnp.exp(jnp.sum(la[rs], axis=0, keepdims=True))

    heads = [(b, hh) for b in range(nb) for hh in range(GLA_HEADS)]
    for c in range(nchunk):
        kvt = {(b, hh): lax.dot_general(v[rows_of(b, c), vs(hh)], kd[b, c][:, ks(hh)], (((0,), (0,)), ((), ())),
                                        preferred_element_type=F32) for b, hh in heads}
        st = {}
        for b, hh in heads:
            st[b, hh] = dec[b, c][:, ks(hh)] * st_ref[b, hh] + kvt[b, hh]
            st_ref[b, hh] = st[b, hh]
        for b, hh in heads:
            obuf_ref[rows_of(b, c), vs(hh)] = lax.dot_general(
                q[rows_of(b, c), ks(hh)], st[b, hh].astype(BF16), (((1,), (1,)), ((), ())),
                preferred_element_type=F32)

    outs = []
    for hh in range(GLA_HEADS):
        oh = obuf_ref[:, vs(hh)]
        outs.append(oh * lax.rsqrt(jnp.mean(oh * oh, axis=-1, keepdims=True) + EPS))
    o = jnp.concatenate(outs, axis=1) * gn_ref[...]
    o = (o * (r * jax.nn.sigmoid(r))).astype(BF16)
    y = jnp.dot(o, wout_ref[...], preferred_element_type=F32)
    for b in range(nb):
        o_ref[b] = x_ref[b] + mod_ref[b, 2:3, :] * y[b * tm:(b + 1) * tm]


def _gla_layer(x, mod, g_mix, w_in_p, w_g2_p, b_gate, g_norm, w_out):
    nb, seq, d = x.shape
    tm = GLA_TM
    return pl.pallas_call(
        functools.partial(_gla_kernel, tm=tm, nb=nb),
        grid=(seq // tm,),
        in_specs=[
            pl.BlockSpec((nb, tm, d), lambda i: (0, i, 0)),
            _const_spec((nb, 6, d)),
            _const_spec((1, d)),
            _const_spec(w_in_p.shape),
            _const_spec(w_g2_p.shape),
            _const_spec((1, GLA_QK)),
            _const_spec((1, d)),
            _const_spec(w_out.shape),
        ],
        out_specs=pl.BlockSpec((nb, tm, d), lambda i: (0, i, 0)),
        out_shape=jax.ShapeDtypeStruct(x.shape, F32),
        scratch_shapes=[
            pltpu.VMEM((nb, GLA_HEADS, GLA_DV, GLA_DK), F32),
            pltpu.VMEM((nb * tm, d), F32),
        ],
        compiler_params=_cparams(("arbitrary",)),
        name="gla_layer",
    )(x, mod, g_mix.reshape(1, d), w_in_p, w_g2_p, b_gate.reshape(1, GLA_QK), g_norm.reshape(1, d), w_out)


def kernel(x, c, w_ada, b_ada, norm_mix, norm_mlp, s5_a_re, s5_a_im, s5_log_dt, s5_b_re, s5_b_im,
           s5_c_re, s5_c_im, s5_d, s5_w_glu, gla_w_in, gla_w_gate2, gla_b_gate, gla_g_norm, gla_w_out,
           w_ff1, w_ff2, norm_final):
    bsz = x.shape[0]
    mod = _adaln(c, w_ada, b_ada).reshape(DEPTH, bsz, 6, D_MODEL)

    prep = _s5_prep(s5_a_re[0], s5_a_im[0], s5_log_dt[0], s5_b_re[0], s5_b_im[0], s5_c_re[0], s5_c_im[0])
    wb, wc, lam2 = _s5_layout(prep)
    x = _s5_layer(x, mod[0], norm_mix[0], wb, wc, lam2, s5_d[0], s5_w_glu[0].astype(BF16))
    x = _mlp_layer(x, mod[0], norm_mlp[0], w_ff1[0].astype(BF16), w_ff2[0].astype(BF16), norm_final, False)

    w_in = gla_w_in[0]
    o_v = 2 * GLA_QK
    o_g = o_v + D_MODEL
    o_r = o_g + GLA_GATE_RANK
    w_in_p = jnp.concatenate(
        [w_in[:, :o_g], w_in[:, o_r:], w_in[:, o_g:o_r],
         jnp.zeros((D_MODEL, GLA_GLR_PAD - GLA_GATE_RANK), w_in.dtype)], axis=1).astype(BF16)
    w_g2_p = jnp.concatenate(
        [gla_w_gate2[0], jnp.zeros((GLA_GLR_PAD - GLA_GATE_RANK, GLA_QK), gla_w_gate2.dtype)], axis=0).astype(BF16)
    x = _gla_layer(x, mod[1], norm_mix[1], w_in_p, w_g2_p, gla_b_gate[0], gla_g_norm[0],
                   gla_w_out[0].astype(BF16))
    x = _mlp_layer(x, mod[1], norm_mlp[1], w_ff1[1].astype(BF16), w_ff2[1].astype(BF16), norm_final, True)
    return x
```

```python
import functools

import jax
import jax.numpy as jnp
from jax import lax
from jax.experimental import pallas as pl
from jax.experimental.pallas import tpu as pltpu

F32 = jnp.float32
BF16 = jnp.bfloat16

D_MODEL = 1024
DEPTH = 2
CHUNK = 64
S5_GROUP = 16
S5_GROUPS = D_MODEL // S5_GROUP
S5_STATE = 64
GLA_HEADS = 4
GLA_QK = D_MODEL // 2
GLA_DK = GLA_QK // GLA_HEADS
GLA_DV = D_MODEL // GLA_HEADS
GLA_GATE_RANK = 16
GLA_GATE_TAU = 16.0
EPS = 1e-6

LANES = 128
SUBLANES = 8
VMEM_LIMIT_BYTES = 56 * 1024 * 1024

S5_SLABS = D_MODEL // LANES
S5_SLAB_GROUPS = LANES // S5_GROUP
S5_SLAB_COLS = S5_SLAB_GROUPS * 2 * S5_STATE
S5_OCTETS = S5_SLABS // 2
S5_OCTET_COLS = 2 * S5_SLAB_COLS
S5_TILES = S5_OCTET_COLS // (2 * LANES)
S5_COLS = S5_SLABS * S5_SLAB_COLS

S5_TM = 128
MLP_TM = 512
GLA_TM = 128

GLA_GLR_PAD = LANES


def _cparams(sem):
    return pltpu.CompilerParams(dimension_semantics=sem, vmem_limit_bytes=VMEM_LIMIT_BYTES)


def _const_spec(shape):
    nd = len(shape)
    return pl.BlockSpec(shape, lambda *_: (0,) * nd, pipeline_mode=pl.Buffered(1))


def _rms(x, g):
    return x * lax.rsqrt(jnp.mean(x * x, axis=-1, keepdims=True) + EPS) * g


def _adaln_kernel(c_ref, w_ref, b_ref, o_ref):
    c = c_ref[...]
    cs = (c * jax.nn.sigmoid(c)).astype(BF16)
    o_ref[0] = jnp.dot(cs, w_ref[0].astype(BF16), preferred_element_type=F32) + b_ref[0]


def _adaln(c, w_ada, b_ada):
    nb = c.shape[0]
    bsz = -(-nb // SUBLANES) * SUBLANES
    c = jnp.pad(c, ((0, bsz - nb), (0, 0)))
    n = w_ada.shape[-1]
    tn = 1536
    out = pl.pallas_call(
        _adaln_kernel,
        grid=(DEPTH, n // tn),
        in_specs=[
            pl.BlockSpec((bsz, D_MODEL), lambda i, j: (0, 0)),
            pl.BlockSpec((1, D_MODEL, tn), lambda i, j: (i, 0, j)),
            pl.BlockSpec((1, 1, tn), lambda i, j: (i, 0, j)),
        ],
        out_specs=pl.BlockSpec((1, bsz, tn), lambda i, j: (i, 0, j)),
        out_shape=jax.ShapeDtypeStruct((DEPTH, bsz, n), F32),
        compiler_params=_cparams(("arbitrary", "arbitrary")),
        name="adaln",
    )(c, w_ada, b_ada.reshape(DEPTH, 1, n))
    return out[:, :nb]


def _s5_prep_kernel(ar_ref, ai_ref, ldt_ref, br_ref, bi_ref, cr_ref, ci_ref,
                    lam2r_ref, lam2i_ref, bbr_ref, bbi_ref, lbr_ref, lbi_ref, cro_ref, cio_ref):
    dt = jnp.exp(ldt_ref[...])
    ar = ar_ref[...]
    ai = ai_ref[...]
    mag = jnp.exp(ar * dt)
    ph = ai * dt
    lre = mag * jnp.cos(ph)
    lim = mag * jnp.sin(ph)
    den = ar * ar + ai * ai
    nr = lre - 1.0
    ni = lim
    fre = (nr * ar + ni * ai) / den
    fim = (ni * ar - nr * ai) / den
    br = br_ref[...]
    bi = bi_ref[...]
    bbr = fre * br - fim * bi
    bbi = fre * bi + fim * br
    bbr_ref[...] = bbr.astype(BF16)
    bbi_ref[...] = bbi.astype(BF16)
    lbr_ref[...] = (lre * bbr - lim * bbi).astype(BF16)
    lbi_ref[...] = (lre * bbi + lim * bbr).astype(BF16)
    lam2r_ref[...] = lre * lre - lim * lim
    lam2i_ref[...] = 2.0 * (lre * lim)
    cro_ref[...] = cr_ref[...].astype(BF16)
    cio_ref[...] = (-ci_ref[...]).astype(BF16)


def _s5_prep(a_re, a_im, log_dt, b_re, b_im, c_re, c_im):
    rows = S5_GROUPS * S5_GROUP
    rep = lambda a: jnp.repeat(a, S5_GROUP, axis=0)
    ar = rep(a_re)
    ai = rep(a_im)
    ldt = rep(jnp.broadcast_to(log_dt[:, None], (S5_GROUPS, S5_STATE)))
    bt = lambda b: jnp.transpose(b, (0, 2, 1)).reshape(rows, S5_STATE)
    cc = lambda c: c.reshape(rows, S5_STATE)
    shp = (rows, S5_STATE)
    spec = pl.BlockSpec(shp, lambda: (0, 0))
    outs = pl.pallas_call(
        _s5_prep_kernel,
        in_specs=[spec] * 7,
        out_specs=[spec] * 8,
        out_shape=[jax.ShapeDtypeStruct(shp, F32)] * 2 + [jax.ShapeDtypeStruct(shp, BF16)] * 6,
        name="s5_prep",
    )(ar, ai, ldt, bt(b_re), bt(b_im), cc(c_re), cc(c_im))
    return outs


def _s5_layout(prep):
    lam2r, lam2i, bbr, bbi, lbr, lbi, cr, nci = prep
    g, h, p = S5_GROUPS, S5_GROUP, S5_STATE
    sg = S5_SLAB_GROUPS
    og = 2 * sg
    ar = jnp.arange
    sel_b = ar(sg)[:, None, None] == 2 * ar(sg // 2)[None, :, None] + ar(2)[None, None, :]

    def taps(re, im):
        return jnp.stack([re.reshape(S5_SLABS, sg, h, p), im.reshape(S5_SLABS, sg, h, p)], axis=3)
    v = jnp.stack([taps(bbr, bbi), taps(lbr, lbi)], axis=1)
    wb = jnp.where(sel_b[None, None, :, None, :, None, :, None], v[:, :, :, :, None, :, None, :],
                   jnp.zeros((), v.dtype))
    wb = wb.reshape(S5_SLABS, 2 * sg * h, S5_SLAB_COLS)

    def ct(a):
        return jnp.transpose(a.reshape(S5_OCTETS, og, h, p), (0, 3, 1, 2))
    c = jnp.stack([ct(cr), ct(nci)], axis=1)
    sel_c = (ar(og)[None, None, None, :]
             == sg * ar(2)[:, None, None, None] + 2 * ar(sg // 2)[None, :, None, None] + ar(2)[None, None, :, None])
    wc = jnp.where(sel_c[None, :, :, None, :, None, :, None], c[:, None, None, :, None, :, :, :],
                   jnp.zeros((), c.dtype))
    wc = wc.reshape(S5_OCTETS, S5_OCTET_COLS, og * h)

    def cols(v):
        return v.reshape(g, h, p)[:, 0, :].reshape(S5_SLABS, sg // 2, 2, p)
    lam2 = jnp.stack([cols(lam2r), cols(lam2i)], axis=2)
    lam2 = jnp.broadcast_to(lam2.reshape(1, S5_COLS), (SUBLANES, S5_COLS))
    return wb, wc, lam2


def _s5_kernel(x_ref, mod_ref, g_ref, wb_ref, wc_ref, lam2_ref, dskip_ref, wglu_ref, o_ref,
               lhs_ref, lhs2_ref, w_ref, carry_ref, y_ref, mix_ref, *, tm, nb):
    rows = nb * tm
    step = pl.program_id(0)

    @pl.when(step == 0)
    def _():
        carry_ref[...] = jnp.zeros_like(carry_ref)
        lhs2_ref[...] = jnp.zeros_like(lhs2_ref)

    @pl.when(step > 0)
    def _():
        lhs2_ref[:, 0:SUBLANES, :] = lhs2_ref[:, rows:rows + SUBLANES, :]

    g = g_ref[...]
    for b in range(nb):
        sh = mod_ref[b, 0:1, :]
        sc = mod_ref[b, 1:2, :]
        h = _rms(x_ref[b], g) * (1.0 + sc) + sh
        for s in range(S5_SLABS):
            hs = h[:, s * LANES:(s + 1) * LANES]
            lhs_ref[s, pl.ds(b, tm, stride=nb), :] = hs
            lhs2_ref[s, pl.ds(nb + b, tm, stride=nb), :] = hs

    for n in range(S5_OCTETS):
        slot = n % 2
        for sl in range(2):
            s = 2 * n + sl
            a = jnp.concatenate([lhs_ref[s], lhs2_ref[s, 0:rows, :]], axis=1).astype(BF16)
            w_ref[slot, :, sl * S5_SLAB_COLS:(sl + 1) * S5_SLAB_COLS] = jnp.dot(
                a, wb_ref[s], preferred_element_type=F32)

        base = n * S5_OCTET_COLS
        for i in range(S5_TILES):
            cr = slice(base + (2 * i) * LANES, base + (2 * i + 1) * LANES)
            ci = slice(base + (2 * i + 1) * LANES, base + (2 * i + 2) * LANES)
            lr, li = lam2_ref[:, cr], lam2_ref[:, ci]
            yr, yi = carry_ref[:, cr], carry_ref[:, ci]
            wr_c = slice((2 * i) * LANES, (2 * i + 1) * LANES)
            wi_c = slice((2 * i + 1) * LANES, (2 * i + 2) * LANES)
            for k in range(rows // SUBLANES):
                rk = slice(k * SUBLANES, (k + 1) * SUBLANES)
                nr = lr * yr - li * yi + w_ref[slot, rk, wr_c]
                ni = lr * yi + li * yr + w_ref[slot, rk, wi_c]
                w_ref[slot, rk, wr_c] = nr
                w_ref[slot, rk, wi_c] = ni
                yr, yi = nr, ni
            carry_ref[:, cr] = yr
            carry_ref[:, ci] = yi

        y_ref[:, n * 2 * LANES:(n + 1) * 2 * LANES] = jnp.dot(
            w_ref[slot].astype(BF16), wc_ref[n], preferred_element_type=F32)

    u = jnp.concatenate([lhs_ref[s] for s in range(S5_SLABS)], axis=1)
    y = y_ref[...] + dskip_ref[...] * u
    z = jax.nn.gelu(y).astype(BF16)
    gl = jnp.dot(z, wglu_ref[...], preferred_element_type=F32)
    mix = gl[:, :D_MODEL] * jax.nn.sigmoid(gl[:, D_MODEL:])
    for s in range(S5_SLABS):
        mix_ref[s] = mix[:, s * LANES:(s + 1) * LANES]
    for b in range(nb):
        yb = jnp.concatenate([mix_ref[s, pl.ds(b, tm, stride=nb), :] for s in range(S5_SLABS)], axis=1)
        o_ref[b] = x_ref[b] + mod_ref[b, 2:3, :] * yb


def _s5_layer(x, mod, g_mix, wb, wc, lam2, d_skip, w_glu):
    nb, seq, d = x.shape
    tm = S5_TM
    rows = nb * tm
    kern = functools.partial(_s5_kernel, tm=tm, nb=nb)
    return pl.pallas_call(
        kern,
        grid=(seq // tm,),
        in_specs=[
            pl.BlockSpec((nb, tm, d), lambda i: (0, i, 0)),
            _const_spec((nb, 6, d)),
            _const_spec((1, d)),
            _const_spec(wb.shape),
            _const_spec(wc.shape),
            _const_spec(lam2.shape),
            _const_spec((1, d)),
            _const_spec(w_glu.shape),
        ],
        out_specs=pl.BlockSpec((nb, tm, d), lambda i: (0, i, 0)),
        out_shape=jax.ShapeDtypeStruct(x.shape, F32),
        scratch_shapes=[
            pltpu.VMEM((S5_SLABS, rows, LANES), F32),
            pltpu.VMEM((S5_SLABS, rows + SUBLANES, LANES), F32),
            pltpu.VMEM((2, rows, S5_OCTET_COLS), F32),
            pltpu.VMEM((SUBLANES, S5_COLS), F32),
            pltpu.VMEM((rows, d), F32),
            pltpu.VMEM((S5_SLABS, rows, LANES), F32),
        ],
        compiler_params=_cparams(("arbitrary",)),
        name="s5_layer",
    )(x, mod, g_mix.reshape(1, d), wb, wc, lam2, d_skip.reshape(1, d), w_glu)


def _mlp_kernel(x_ref, mod_ref, g_ref, w1_ref, w2_ref, gf_ref, o_ref, *, final):
    x = x_ref[...]
    sh = mod_ref[0, 3:4, :]
    sc = mod_ref[0, 4:5, :]
    gt = mod_ref[0, 5:6, :]
    h = (_rms(x, g_ref[...]) * (1.0 + sc) + sh).astype(BF16)
    a = jnp.dot(h, w1_ref[0], preferred_element_type=F32)
    a = jnp.maximum(a, 0.0)
    a = (a * a).astype(BF16)
    y = jnp.dot(a, w2_ref[0], preferred_element_type=F32)
    out = x + gt * y
    if final:
        out = _rms(out, gf_ref[...])
    o_ref[...] = out


def _mlp_layer(x, mod, g_mlp, w1, w2, layer, g_final, final):
    nb, seq, d = x.shape
    tm = MLP_TM
    per_b = seq // tm
    x2 = x.reshape(nb * seq, d)
    layer_spec = lambda w: pl.BlockSpec((1,) + w.shape[1:], lambda i: (layer, 0, 0), pipeline_mode=pl.Buffered(1))
    out = pl.pallas_call(
        functools.partial(_mlp_kernel, final=final),
        grid=(nb * per_b,),
        in_specs=[
            pl.BlockSpec((tm, d), lambda i: (i, 0)),
            pl.BlockSpec((1, 6, d), lambda i: (i // per_b, 0, 0)),
            _const_spec((1, d)),
            layer_spec(w1),
            layer_spec(w2),
            _const_spec((1, d)),
        ],
        out_specs=pl.BlockSpec((tm, d), lambda i: (i, 0)),
        out_shape=jax.ShapeDtypeStruct(x2.shape, F32),
        compiler_params=_cparams(("arbitrary",)),
        name="mlp_final" if final else "mlp_layer",
    )(x2, mod, g_mlp.reshape(1, d), w1, w2, g_final.reshape(1, d))
    return out.reshape(nb, seq, d)


def _gla_kernel(x_ref, mod_ref, g_ref, wqkv_ref, wr_ref, wglr_ref, wg2_ref, bg_ref, gn_ref, wout_ref, o_ref,
                st_ref, obuf_ref, *, tm, nb):
    @pl.when(pl.program_id(0) == 0)
    def _():
        st_ref[...] = jnp.zeros_like(st_ref)

    g = g_ref[...]
    h = jnp.concatenate(
        [(_rms(x_ref[b], g) * (1.0 + mod_ref[b, 1:2, :]) + mod_ref[b, 0:1, :]).astype(BF16) for b in range(nb)],
        axis=0)
    proj = jnp.dot(h, wqkv_ref[...], preferred_element_type=F32)
    o_q, o_k, o_v = 0, GLA_QK, 2 * GLA_QK
    q = (proj[:, o_q:o_k] * (GLA_DK ** -0.5)).astype(BF16)
    k = proj[:, o_k:o_v]
    v = proj[:, o_v:].astype(BF16)
    r = jnp.dot(h, wr_ref[...], preferred_element_type=F32)
    glr = jnp.dot(h, wglr_ref[...], preferred_element_type=F32).astype(BF16)
    pre = jnp.dot(glr, wg2_ref[...], preferred_element_type=F32) + bg_ref[...]
    la = (jnp.minimum(pre, 0.0) - jnp.log(1.0 + jnp.exp(-jnp.abs(pre)))) * (1.0 / GLA_GATE_TAU)
    la_hi = la.astype(BF16)
    la_lo = (la - la_hi.astype(F32)).astype(BF16)
    la2 = jnp.concatenate([la_hi, la_lo], axis=1)

    ti = lax.broadcasted_iota(jnp.int32, (CHUNK, CHUNK), 0)
    si = lax.broadcasted_iota(jnp.int32, (CHUNK, CHUNK), 1)
    upper = (si > ti).astype(BF16)

    nchunk = tm // CHUNK
    rows_of = lambda b, c: slice(b * tm + c * CHUNK, b * tm + (c + 1) * CHUNK)
    ks = lambda hh: slice(hh * GLA_DK, (hh + 1) * GLA_DK)
    vs = lambda hh: slice(hh * GLA_DV, (hh + 1) * GLA_DV)
    kd, dec = {}, {}
    for c in range(nchunk):
        for b in range(nb):
            rs = rows_of(b, c)
            rc2 = jnp.dot(upper, la2[rs], preferred_element_type=F32)
            rc = rc2[:, :GLA_QK] + rc2[:, GLA_QK:]
            kd[b, c] = (k[rs] * jnp.exp(rc)).astype(BF16)
            dec[b, c] = jnp.exp(jnp.sum(la[rs], axis=0, keepdims=True))

    heads = [(b, hh) for b in range(nb) for hh in range(GLA_HEADS)]
    for c in range(nchunk):
        kvt = {(b, hh): lax.dot_general(v[rows_of(b, c), vs(hh)], kd[b, c][:, ks(hh)], (((0,), (0,)), ((), ())),
                                        preferred_element_type=F32) for b, hh in heads}
        st = {}
        for b, hh in heads:
            st[b, hh] = dec[b, c][:, ks(hh)] * st_ref[b, hh] + kvt[b, hh]
            st_ref[b, hh] = st[b, hh]
        for b, hh in heads:
            obuf_ref[rows_of(b, c), vs(hh)] = lax.dot_general(
                q[rows_of(b, c), ks(hh)], st[b, hh].astype(BF16), (((1,), (1,)), ((), ())),
                preferred_element_type=F32)

    outs = []
    for hh in range(GLA_HEADS):
        oh = obuf_ref[:, vs(hh)]
        outs.append(oh * lax.rsqrt(jnp.mean(oh * oh, axis=-1, keepdims=True) + EPS))
    o = jnp.concatenate(outs, axis=1) * gn_ref[...]
    o = (o * (r * jax.nn.sigmoid(r))).astype(BF16)
    y = jnp.dot(o, wout_ref[...], preferred_element_type=F32)
    for b in range(nb):
        o_ref[b] = x_ref[b] + mod_ref[b, 2:3, :] * y[b * tm:(b + 1) * tm]


def _gla_layer(x, mod, g_mix, w_qkv, w_r, w_glr, w_g2_p, b_gate, g_norm, w_out):
    nb, seq, d = x.shape
    tm = GLA_TM
    return pl.pallas_call(
        functools.partial(_gla_kernel, tm=tm, nb=nb),
        grid=(seq // tm,),
        in_specs=[
            pl.BlockSpec((nb, tm, d), lambda i: (0, i, 0)),
            _const_spec((nb, 6, d)),
            _const_spec((1, d)),
            _const_spec(w_qkv.shape),
            _const_spec(w_r.shape),
            _const_spec(w_glr.shape),
            _const_spec(w_g2_p.shape),
            _const_spec((1, GLA_QK)),
            _const_spec((1, d)),
            _const_spec(w_out.shape),
        ],
        out_specs=pl.BlockSpec((nb, tm, d), lambda i: (0, i, 0)),
        out_shape=jax.ShapeDtypeStruct(x.shape, F32),
        scratch_shapes=[
            pltpu.VMEM((nb, GLA_HEADS, GLA_DV, GLA_DK), F32),
            pltpu.VMEM((nb * tm, d), F32),
        ],
        compiler_params=_cparams(("arbitrary",)),
        name="gla_layer",
    )(x, mod, g_mix.reshape(1, d), w_qkv, w_r, w_glr, w_g2_p, b_gate.reshape(1, GLA_QK), g_norm.reshape(1, d),
      w_out)


def kernel(x, c, w_ada, b_ada, norm_mix, norm_mlp, s5_a_re, s5_a_im, s5_log_dt, s5_b_re, s5_b_im,
           s5_c_re, s5_c_im, s5_d, s5_w_glu, gla_w_in, gla_w_gate2, gla_b_gate, gla_g_norm, gla_w_out,
           w_ff1, w_ff2, norm_final):
    bsz = x.shape[0]
    mod = _adaln(c, w_ada, b_ada).reshape(DEPTH, bsz, 6, D_MODEL)

    prep = _s5_prep(s5_a_re[0], s5_a_im[0], s5_log_dt[0], s5_b_re[0], s5_b_im[0], s5_c_re[0], s5_c_im[0])
    wb, wc, lam2 = _s5_layout(prep)
    x = _s5_layer(x, mod[0], norm_mix[0], wb, wc, lam2, s5_d[0], s5_w_glu[0].astype(BF16))
    w1 = w_ff1.astype(BF16)
    w2 = w_ff2.astype(BF16)
    x = _mlp_layer(x, mod[0], norm_mlp[0], w1, w2, 0, norm_final, False)

    w_in = gla_w_in[0]
    o_g = 2 * GLA_QK + D_MODEL
    o_r = o_g + GLA_GATE_RANK
    w_qkv = w_in[:, :o_g].astype(BF16)
    w_r = w_in[:, o_r:].astype(BF16)
    w_glr = jnp.pad(w_in[:, o_g:o_r], ((0, 0), (0, GLA_GLR_PAD - GLA_GATE_RANK))).astype(BF16)
    w_g2_p = jnp.pad(gla_w_gate2[0], ((0, GLA_GLR_PAD - GLA_GATE_RANK), (0, 0))).astype(BF16)
    x = _gla_layer(x, mod[1], norm_mix[1], w_qkv, w_r, w_glr, w_g2_p, gla_b_gate[0], gla_g_norm[0],
                   gla_w_out[0].astype(BF16))
    x = _mlp_layer(x, mod[1], norm_mlp[1], w1, w2, 1, norm_final, True)
    return x
```

```python
import functools

import jax
import jax.numpy as jnp
from jax import lax
from jax.experimental import pallas as pl
from jax.experimental.pallas import tpu as pltpu

F32 = jnp.float32
BF16 = jnp.bfloat16

D_MODEL = 1024
DEPTH = 2
CHUNK = 64
S5_GROUP = 16
S5_GROUPS = D_MODEL // S5_GROUP
S5_STATE = 64
GLA_HEADS = 4
GLA_QK = D_MODEL // 2
GLA_DK = GLA_QK // GLA_HEADS
GLA_DV = D_MODEL // GLA_HEADS
GLA_GATE_RANK = 16
GLA_GATE_TAU = 16.0
EPS = 1e-6

LANES = 128
SUBLANES = 8
VMEM_LIMIT_BYTES = 56 * 1024 * 1024

S5_SLABS = D_MODEL // LANES
S5_SLAB_GROUPS = LANES // S5_GROUP
S5_SLAB_COLS = S5_SLAB_GROUPS * 2 * S5_STATE
S5_OCTETS = S5_SLABS // 2
S5_OCTET_COLS = 2 * S5_SLAB_COLS
S5_TILES = S5_OCTET_COLS // (2 * LANES)
S5_COLS = S5_SLABS * S5_SLAB_COLS

S5_TM = 128
MLP_TM = 512
GLA_TM = 128

GLA_GLR_PAD = LANES


def _cparams(sem):
    return pltpu.CompilerParams(dimension_semantics=sem, vmem_limit_bytes=VMEM_LIMIT_BYTES)


def _const_spec(shape):
    nd = len(shape)
    return pl.BlockSpec(shape, lambda *_: (0,) * nd, pipeline_mode=pl.Buffered(1))


def _rms(x, g):
    return x * lax.rsqrt(jnp.mean(x * x, axis=-1, keepdims=True) + EPS) * g


def _adaln_kernel(c_ref, w_ref, b_ref, o_ref):
    c = c_ref[...]
    cs = (c * jax.nn.sigmoid(c)).astype(BF16)
    o_ref[0] = jnp.dot(cs, w_ref[0].astype(BF16), preferred_element_type=F32) + b_ref[0]


def _adaln(c, w_ada, b_ada):
    nb = c.shape[0]
    bsz = -(-nb // SUBLANES) * SUBLANES
    c = jnp.pad(c, ((0, bsz - nb), (0, 0)))
    n = w_ada.shape[-1]
    tn = 1536
    out = pl.pallas_call(
        _adaln_kernel,
        grid=(DEPTH, n // tn),
        in_specs=[
            pl.BlockSpec((bsz, D_MODEL), lambda i, j: (0, 0)),
            pl.BlockSpec((1, D_MODEL, tn), lambda i, j: (i, 0, j)),
            pl.BlockSpec((1, 1, tn), lambda i, j: (i, 0, j)),
        ],
        out_specs=pl.BlockSpec((1, bsz, tn), lambda i, j: (i, 0, j)),
        out_shape=jax.ShapeDtypeStruct((DEPTH, bsz, n), F32),
        compiler_params=_cparams(("arbitrary", "arbitrary")),
        name="adaln",
    )(c, w_ada, b_ada.reshape(DEPTH, 1, n))
    return out[:, :nb]


def _s5_prep_kernel(ar_ref, ai_ref, ldt_ref, br_ref, bi_ref, crt_ref, cit_ref,
                    lam2r_ref, lam2i_ref, wb_ref, wc_ref):
    dt = jnp.exp(ldt_ref[...])
    ar = ar_ref[...]
    ai = ai_ref[...]
    mag = jnp.exp(ar * dt)
    ph = ai * dt
    lre = mag * jnp.cos(ph)
    lim = mag * jnp.sin(ph)
    den = ar * ar + ai * ai
    nr = lre - 1.0
    ni = lim
    fre = (nr * ar + ni * ai) / den
    fim = (ni * ar - nr * ai) / den
    br = br_ref[...]
    bi = bi_ref[...]
    bbr = fre * br - fim * bi
    bbi = fre * bi + fim * br
    lbr = lre * bbr - lim * bbi
    lbi = lre * bbi + lim * bbr
    lam2r_ref[...] = lre * lre - lim * lim
    lam2i_ref[...] = 2.0 * (lre * lim)

    rows = S5_GROUPS * S5_GROUP
    row = lax.broadcasted_iota(jnp.int32, (rows, LANES), 0)
    lane = lax.broadcasted_iota(jnp.int32, (rows, LANES), 1)
    grp = row // S5_GROUP
    own_half = (lane // S5_STATE) == (grp % 2)
    tile_of_row = (grp % S5_SLAB_GROUPS) // 2
    slab_rows = S5_SLAB_GROUPS * S5_GROUP
    for tap, (vr, vi) in enumerate(((bbr, bbi), (lbr, lbi))):
        for i in range(S5_TILES // 2):
            keep = own_half & (tile_of_row == i)
            for ri, v in enumerate((vr, vi)):
                blk = jnp.where(keep, v, 0.0).astype(BF16)
                col = (2 * i + ri) * LANES
                for s in range(S5_SLABS):
                    wb_ref[s, tap * slab_rows:(tap + 1) * slab_rows, col:col + LANES] = (
                        blk[s * slab_rows:(s + 1) * slab_rows])

    og = 2 * S5_SLAB_GROUPS
    gcol = lax.broadcasted_iota(jnp.int32, (S5_STATE, og * S5_GROUP), 1) // S5_GROUP
    for n in range(S5_OCTETS):
        c_n = (crt_ref[n], -cit_ref[n])
        for sl in range(2):
            for i in range(S5_TILES // 2):
                for ri in range(2):
                    for par in range(2):
                        r0 = (((sl * (S5_TILES // 2) + i) * 2 + ri) * 2 + par) * S5_STATE
                        g_own = sl * S5_SLAB_GROUPS + 2 * i + par
                        wc_ref[n, r0:r0 + S5_STATE, :] = jnp.where(gcol == g_own, c_n[ri], 0.0).astype(BF16)


def _s5_prep(a_re, a_im, log_dt, b_re, b_im, c_re, c_im):
    rows = S5_GROUPS * S5_GROUP
    g, h, p = S5_GROUPS, S5_GROUP, S5_STATE
    og = 2 * S5_SLAB_GROUPS
    twice = lambda a: jnp.concatenate([a, a], axis=1)
    rep = lambda a: twice(jnp.repeat(a, h, axis=0))
    bt = lambda b: twice(jnp.transpose(b, (0, 2, 1)).reshape(rows, p))
    ct = lambda c: jnp.transpose(c.reshape(S5_OCTETS, og, h, p), (0, 3, 1, 2)).reshape(S5_OCTETS, p, og * h)
    ldt = jnp.broadcast_to(log_dt[:, None], (g, p))
    shp = (rows, 2 * p)
    full = lambda s: pl.BlockSpec(s, lambda: (0,) * len(s))
    cshp = (S5_OCTETS, p, og * h)
    wb_shp = (S5_SLABS, 2 * S5_SLAB_GROUPS * h, S5_SLAB_COLS)
    wc_shp = (S5_OCTETS, S5_OCTET_COLS, og * h)
    lam2r, lam2i, wb, wc = pl.pallas_call(
        _s5_prep_kernel,
        in_specs=[full(shp)] * 5 + [full(cshp)] * 2,
        out_specs=[full(shp), full(shp), full(wb_shp), full(wc_shp)],
        out_shape=[jax.ShapeDtypeStruct(shp, F32)] * 2
        + [jax.ShapeDtypeStruct(wb_shp, BF16), jax.ShapeDtypeStruct(wc_shp, BF16)],
        name="s5_prep",
    )(rep(a_re), rep(a_im), rep(ldt), bt(b_re), bt(b_im), ct(c_re), ct(c_im))

    def cols(v):
        return v[:, :p].reshape(g, h, p)[:, 0, :].reshape(S5_SLABS, S5_SLAB_GROUPS // 2, 2, p)
    lam2 = jnp.stack([cols(lam2r), cols(lam2i)], axis=2)
    lam2 = jnp.broadcast_to(lam2.reshape(1, S5_COLS), (SUBLANES, S5_COLS))
    return wb, wc, lam2


def _s5_kernel(x_ref, mod_ref, g_ref, wb_ref, wc_ref, lam2_ref, dskip_ref, wglu_ref, o_ref,
               lhs_ref, lhs2_ref, w_ref, carry_ref, y_ref, mix_ref, *, tm, nb):
    rows = nb * tm
    step = pl.program_id(0)

    @pl.when(step == 0)
    def _():
        carry_ref[...] = jnp.zeros_like(carry_ref)
        lhs2_ref[...] = jnp.zeros_like(lhs2_ref)

    @pl.when(step > 0)
    def _():
        lhs2_ref[:, 0:SUBLANES, :] = lhs2_ref[:, rows:rows + SUBLANES, :]

    g = g_ref[...]
    for b in range(nb):
        sh = mod_ref[b, 0:1, :]
        sc = mod_ref[b, 1:2, :]
        h = _rms(x_ref[b], g) * (1.0 + sc) + sh
        for s in range(S5_SLABS):
            hs = h[:, s * LANES:(s + 1) * LANES]
            lhs_ref[s, pl.ds(b, tm, stride=nb), :] = hs
            lhs2_ref[s, pl.ds(nb + b, tm, stride=nb), :] = hs

    for n in range(S5_OCTETS):
        slot = n % 2
        for sl in range(2):
            s = 2 * n + sl
            a = jnp.concatenate([lhs_ref[s], lhs2_ref[s, 0:rows, :]], axis=1).astype(BF16)
            w_ref[slot, :, sl * S5_SLAB_COLS:(sl + 1) * S5_SLAB_COLS] = jnp.dot(
                a, wb_ref[s], preferred_element_type=F32)

        base = n * S5_OCTET_COLS
        for i in range(S5_TILES):
            cr = slice(base + (2 * i) * LANES, base + (2 * i + 1) * LANES)
            ci = slice(base + (2 * i + 1) * LANES, base + (2 * i + 2) * LANES)
            lr, li = lam2_ref[:, cr], lam2_ref[:, ci]
            yr, yi = carry_ref[:, cr], carry_ref[:, ci]
            wr_c = slice((2 * i) * LANES, (2 * i + 1) * LANES)
            wi_c = slice((2 * i + 1) * LANES, (2 * i + 2) * LANES)
            for k in range(rows // SUBLANES):
                rk = slice(k * SUBLANES, (k + 1) * SUBLANES)
                nr = lr * yr - li * yi + w_ref[slot, rk, wr_c]
                ni = lr * yi + li * yr + w_ref[slot, rk, wi_c]
                w_ref[slot, rk, wr_c] = nr
                w_ref[slot, rk, wi_c] = ni
                yr, yi = nr, ni
            carry_ref[:, cr] = yr
            carry_ref[:, ci] = yi

        y_ref[:, n * 2 * LANES:(n + 1) * 2 * LANES] = jnp.dot(
            w_ref[slot].astype(BF16), wc_ref[n], preferred_element_type=F32)

    u = jnp.concatenate([lhs_ref[s] for s in range(S5_SLABS)], axis=1)
    y = y_ref[...] + dskip_ref[...] * u
    z = jax.nn.gelu(y).astype(BF16)
    gl = jnp.dot(z, wglu_ref[...], preferred_element_type=F32)
    mix = gl[:, :D_MODEL] * jax.nn.sigmoid(gl[:, D_MODEL:])
    for s in range(S5_SLABS):
        mix_ref[s] = mix[:, s * LANES:(s + 1) * LANES]
    for b in range(nb):
        yb = jnp.concatenate([mix_ref[s, pl.ds(b, tm, stride=nb), :] for s in range(S5_SLABS)], axis=1)
        o_ref[b] = x_ref[b] + mod_ref[b, 2:3, :] * yb


def _s5_layer(x, mod, g_mix, wb, wc, lam2, d_skip, w_glu):
    nb, seq, d = x.shape
    tm = S5_TM
    rows = nb * tm
    kern = functools.partial(_s5_kernel, tm=tm, nb=nb)
    return pl.pallas_call(
        kern,
        grid=(seq // tm,),
        in_specs=[
            pl.BlockSpec((nb, tm, d), lambda i: (0, i, 0)),
            _const_spec((nb, 6, d)),
            _const_spec((1, d)),
            _const_spec(wb.shape),
            _const_spec(wc.shape),
            _const_spec(lam2.shape),
            _const_spec((1, d)),
            _const_spec(w_glu.shape),
        ],
        out_specs=pl.BlockSpec((nb, tm, d), lambda i: (0, i, 0)),
        out_shape=jax.ShapeDtypeStruct(x.shape, F32),
        scratch_shapes=[
            pltpu.VMEM((S5_SLABS, rows, LANES), F32),
            pltpu.VMEM((S5_SLABS, rows + SUBLANES, LANES), F32),
            pltpu.VMEM((2, rows, S5_OCTET_COLS), F32),
            pltpu.VMEM((SUBLANES, S5_COLS), F32),
            pltpu.VMEM((rows, d), F32),
            pltpu.VMEM((S5_SLABS, rows, LANES), F32),
        ],
        compiler_params=_cparams(("arbitrary",)),
        name="s5_layer",
    )(x, mod, g_mix.reshape(1, d), wb, wc, lam2, d_skip.reshape(1, d), w_glu)


def _mlp_kernel(x_ref, mod_ref, g_ref, w1_ref, w2_ref, gf_ref, o_ref, *, final):
    x = x_ref[...]
    sh = mod_ref[0, 3:4, :]
    sc = mod_ref[0, 4:5, :]
    gt = mod_ref[0, 5:6, :]
    h = (_rms(x, g_ref[...]) * (1.0 + sc) + sh).astype(BF16)
    a = jnp.dot(h, w1_ref[0], preferred_element_type=F32)
    a = jnp.maximum(a, 0.0)
    a = (a * a).astype(BF16)
    y = jnp.dot(a, w2_ref[0], preferred_element_type=F32)
    out = x + gt * y
    if final:
        out = _rms(out, gf_ref[...])
    o_ref[...] = out


def _mlp_layer(x, mod, g_mlp, w1, w2, layer, g_final, final):
    nb, seq, d = x.shape
    tm = MLP_TM
    per_b = seq // tm
    x2 = x.reshape(nb * seq, d)
    layer_spec = lambda w: pl.BlockSpec((1,) + w.shape[1:], lambda i: (layer, 0, 0), pipeline_mode=pl.Buffered(1))
    out = pl.pallas_call(
        functools.partial(_mlp_kernel, final=final),
        grid=(nb * per_b,),
        in_specs=[
            pl.BlockSpec((tm, d), lambda i: (i, 0)),
            pl.BlockSpec((1, 6, d), lambda i: (i // per_b, 0, 0)),
            _const_spec((1, d)),
            layer_spec(w1),
            layer_spec(w2),
            _const_spec((1, d)),
        ],
        out_specs=pl.BlockSpec((tm, d), lambda i: (i, 0)),
        out_shape=jax.ShapeDtypeStruct(x2.shape, F32),
        compiler_params=_cparams(("arbitrary",)),
        name="mlp_final" if final else "mlp_layer",
    )(x2, mod, g_mlp.reshape(1, d), w1, w2, g_final.reshape(1, d))
    return out.reshape(nb, seq, d)


def _gla_kernel(x_ref, mod_ref, g_ref, wqkv_ref, wr_ref, wglr_ref, wg2_ref, bg_ref, gn_ref, wout_ref, o_ref,
                st_ref, obuf_ref, *, tm, nb):
    @pl.when(pl.program_id(0) == 0)
    def _():
        st_ref[...] = jnp.zeros_like(st_ref)

    g = g_ref[...]
    h = jnp.concatenate(
        [(_rms(x_ref[b], g) * (1.0 + mod_ref[b, 1:2, :]) + mod_ref[b, 0:1, :]).astype(BF16) for b in range(nb)],
        axis=0)
    proj = jnp.dot(h, wqkv_ref[...], preferred_element_type=F32)
    o_q, o_k, o_v = 0, GLA_QK, 2 * GLA_QK
    q = (proj[:, o_q:o_k] * (GLA_DK ** -0.5)).astype(BF16)
    k = proj[:, o_k:o_v]
    v = proj[:, o_v:].astype(BF16)
    r = jnp.dot(h, wr_ref[...], preferred_element_type=F32)
    glr = jnp.dot(h, wglr_ref[...], preferred_element_type=F32).astype(BF16)
    pre = jnp.dot(glr, wg2_ref[...], preferred_element_type=F32) + bg_ref[...]
    la = (jnp.minimum(pre, 0.0) - jnp.log(1.0 + jnp.exp(-jnp.abs(pre)))) * (1.0 / GLA_GATE_TAU)
    la_hi = la.astype(BF16)
    la_lo = (la - la_hi.astype(F32)).astype(BF16)
    la2 = jnp.concatenate([la_hi, la_lo], axis=1)

    ti = lax.broadcasted_iota(jnp.int32, (CHUNK, CHUNK), 0)
    si = lax.broadcasted_iota(jnp.int32, (CHUNK, CHUNK), 1)
    upper = (si > ti).astype(BF16)

    nchunk = tm // CHUNK
    rows_of = lambda b, c: slice(b * tm + c * CHUNK, b * tm + (c + 1) * CHUNK)
    ks = lambda hh: slice(hh * GLA_DK, (hh + 1) * GLA_DK)
    vs = lambda hh: slice(hh * GLA_DV, (hh + 1) * GLA_DV)
    kd, dec = {}, {}
    for c in range(nchunk):
        for b in range(nb):
            rs = rows_of(b, c)
            rc2 = jnp.dot(upper, la2[rs], preferred_element_type=F32)
            rc = rc2[:, :GLA_QK] + rc2[:, GLA_QK:]
            kd[b, c] = (k[rs] * jnp.exp(rc)).astype(BF16)
            dec[b, c] = jnp.exp(jnp.sum(la[rs], axis=0, keepdims=True))

    heads = [(b, hh) for b in range(nb) for hh in range(GLA_HEADS)]
    for c in range(nchunk):
        kvt = {(b, hh): lax.dot_general(v[rows_of(b, c), vs(hh)], kd[b, c][:, ks(hh)], (((0,), (0,)), ((), ())),
                                        preferred_element_type=F32) for b, hh in heads}
        st = {}
        for b, hh in heads:
            st[b, hh] = dec[b, c][:, ks(hh)] * st_ref[b, hh] + kvt[b, hh]
            st_ref[b, hh] = st[b, hh]
        for b, hh in heads:
            obuf_ref[rows_of(b, c), vs(hh)] = lax.dot_general(
                q[rows_of(b, c), ks(hh)], st[b, hh].astype(BF16), (((1,), (1,)), ((), ())),
                preferred_element_type=F32)

    outs = []
    for hh in range(GLA_HEADS):
        oh = obuf_ref[:, vs(hh)]
        outs.append(oh * lax.rsqrt(jnp.mean(oh * oh, axis=-1, keepdims=True) + EPS))
    o = jnp.concatenate(outs, axis=1) * gn_ref[...]
    o = (o * (r * jax.nn.sigmoid(r))).astype(BF16)
    y = jnp.dot(o, wout_ref[...], preferred_element_type=F32)
    for b in range(nb):
        o_ref[b] = x_ref[b] + mod_ref[b, 2:3, :] * y[b * tm:(b + 1) * tm]


def _gla_layer(x, mod, g_mix, w_qkv, w_r, w_glr, w_g2_p, b_gate, g_norm, w_out):
    nb, seq, d = x.shape
    tm = GLA_TM
    return pl.pallas_call(
        functools.partial(_gla_kernel, tm=tm, nb=nb),
        grid=(seq // tm,),
        in_specs=[
            pl.BlockSpec((nb, tm, d), lambda i: (0, i, 0)),
            _const_spec((nb, 6, d)),
            _const_spec((1, d)),
            _const_spec(w_qkv.shape),
            _const_spec(w_r.shape),
            _const_spec(w_glr.shape),
            _const_spec(w_g2_p.shape),
            _const_spec((1, GLA_QK)),
            _const_spec((1, d)),
            _const_spec(w_out.shape),
        ],
        out_specs=pl.BlockSpec((nb, tm, d), lambda i: (0, i, 0)),
        out_shape=jax.ShapeDtypeStruct(x.shape, F32),
        scratch_shapes=[
            pltpu.VMEM((nb, GLA_HEADS, GLA_DV, GLA_DK), F32),
            pltpu.VMEM((nb * tm, d), F32),
        ],
        compiler_params=_cparams(("arbitrary",)),
        name="gla_layer",
    )(x, mod, g_mix.reshape(1, d), w_qkv, w_r, w_glr, w_g2_p, b_gate.reshape(1, GLA_QK), g_norm.reshape(1, d),
      w_out)


def kernel(x, c, w_ada, b_ada, norm_mix, norm_mlp, s5_a_re, s5_a_im, s5_log_dt, s5_b_re, s5_b_im,
           s5_c_re, s5_c_im, s5_d, s5_w_glu, gla_w_in, gla_w_gate2, gla_b_gate, gla_g_norm, gla_w_out,
           w_ff1, w_ff2, norm_final):
    bsz = x.shape[0]
    mod = _adaln(c, w_ada, b_ada).reshape(DEPTH, bsz, 6, D_MODEL)

    wb, wc, lam2 = _s5_prep(s5_a_re[0], s5_a_im[0], s5_log_dt[0], s5_b_re[0], s5_b_im[0], s5_c_re[0], s5_c_im[0])
    x = _s5_layer(x, mod[0], norm_mix[0], wb, wc, lam2, s5_d[0], s5_w_glu[0].astype(BF16))
    w1 = w_ff1.astype(BF16)
    w2 = w_ff2.astype(BF16)
    x = _mlp_layer(x, mod[0], norm_mlp[0], w1, w2, 0, norm_final, False)

    w_in = gla_w_in[0]
    o_g = 2 * GLA_QK + D_MODEL
    o_r = o_g + GLA_GATE_RANK
    w_qkv = w_in[:, :o_g].astype(BF16)
    w_r = w_in[:, o_r:].astype(BF16)
    w_glr = jnp.pad(w_in[:, o_g:o_r], ((0, 0), (0, GLA_GLR_PAD - GLA_GATE_RANK))).astype(BF16)
    w_g2_p = jnp.pad(gla_w_gate2[0], ((0, GLA_GLR_PAD - GLA_GATE_RANK), (0, 0))).astype(BF16)
    x = _gla_layer(x, mod[1], norm_mix[1], w_qkv, w_r, w_glr, w_g2_p, gla_b_gate[0], gla_g_norm[0],
                   gla_w_out[0].astype(BF16))
    x = _mlp_layer(x, mod[1], norm_mlp[1], w1, w2, 1, norm_final, True)
    return x
```

```python
import functools

import jax
import jax.numpy as jnp
from jax import lax
from jax.experimental import pallas as pl
from jax.experimental.pallas import tpu as pltpu

F32 = jnp.float32
BF16 = jnp.bfloat16

D_MODEL = 1024
DEPTH = 2
CHUNK = 64
S5_GROUP = 16
S5_GROUPS = D_MODEL // S5_GROUP
S5_STATE = 64
GLA_HEADS = 4
GLA_QK = D_MODEL // 2
GLA_DK = GLA_QK // GLA_HEADS
GLA_DV = D_MODEL // GLA_HEADS
GLA_GATE_RANK = 16
GLA_GATE_TAU = 16.0
EPS = 1e-6

LANES = 128
SUBLANES = 8
VMEM_LIMIT_BYTES = 56 * 1024 * 1024

S5_SLABS = D_MODEL // LANES
S5_SLAB_GROUPS = LANES // S5_GROUP
S5_SLAB_COLS = S5_SLAB_GROUPS * 2 * S5_STATE
S5_OCTETS = S5_SLABS // 2
S5_OCTET_COLS = 2 * S5_SLAB_COLS
S5_TILES = S5_OCTET_COLS // (2 * LANES)
S5_COLS = S5_SLABS * S5_SLAB_COLS

S5_TM = 128
MLP_TM = 512
GLA_TM = 128

GLA_GLR_PAD = LANES


def _cparams(sem):
    return pltpu.CompilerParams(dimension_semantics=sem, vmem_limit_bytes=VMEM_LIMIT_BYTES)


def _const_spec(shape):
    nd = len(shape)
    return pl.BlockSpec(shape, lambda *_: (0,) * nd, pipeline_mode=pl.Buffered(1))


def _rms(x, g):
    return x * lax.rsqrt(jnp.mean(x * x, axis=-1, keepdims=True) + EPS) * g


def _adaln_kernel(c_ref, w_ref, b_ref, o_ref):
    c = c_ref[...]
    cs = (c * jax.nn.sigmoid(c)).astype(BF16)
    o_ref[0] = jnp.dot(cs, w_ref[0].astype(BF16), preferred_element_type=F32) + b_ref[0]


def _adaln(c, w_ada, b_ada):
    nb = c.shape[0]
    bsz = -(-nb // SUBLANES) * SUBLANES
    c = jnp.pad(c, ((0, bsz - nb), (0, 0)))
    n = w_ada.shape[-1]
    tn = 1536
    out = pl.pallas_call(
        _adaln_kernel,
        grid=(DEPTH, n // tn),
        in_specs=[
            pl.BlockSpec((bsz, D_MODEL), lambda i, j: (0, 0)),
            pl.BlockSpec((1, D_MODEL, tn), lambda i, j: (i, 0, j)),
            pl.BlockSpec((1, 1, tn), lambda i, j: (i, 0, j)),
        ],
        out_specs=pl.BlockSpec((1, bsz, tn), lambda i, j: (i, 0, j)),
        out_shape=jax.ShapeDtypeStruct((DEPTH, bsz, n), F32),
        compiler_params=_cparams(("arbitrary", "arbitrary")),
        name="adaln",
    )(c, w_ada, b_ada.reshape(DEPTH, 1, n))
    return out[:, :nb]


def _s5_prep_kernel(ar_ref, ai_ref, ldt_ref, br_ref, bi_ref, crt_ref, cit_ref,
                    lam2r_ref, lam2i_ref, wb_ref, wc_ref):
    dt = jnp.exp(ldt_ref[...])
    ar = ar_ref[...]
    ai = ai_ref[...]
    mag = jnp.exp(ar * dt)
    ph = ai * dt
    lre = mag * jnp.cos(ph)
    lim = mag * jnp.sin(ph)
    den = ar * ar + ai * ai
    nr = lre - 1.0
    ni = lim
    fre = (nr * ar + ni * ai) / den
    fim = (ni * ar - nr * ai) / den
    br = br_ref[...]
    bi = bi_ref[...]
    bbr = fre * br - fim * bi
    bbi = fre * bi + fim * br
    lbr = lre * bbr - lim * bbi
    lbi = lre * bbi + lim * bbr
    lam2r_ref[...] = lre * lre - lim * lim
    lam2i_ref[...] = 2.0 * (lre * lim)

    rows = S5_GROUPS * S5_GROUP
    row = lax.broadcasted_iota(jnp.int32, (rows, LANES), 0)
    lane = lax.broadcasted_iota(jnp.int32, (rows, LANES), 1)
    grp = row // S5_GROUP
    own_half = (lane // S5_STATE) == (grp % 2)
    tile_of_row = (grp % S5_SLAB_GROUPS) // 2
    slab_rows = S5_SLAB_GROUPS * S5_GROUP
    for tap, (vr, vi) in enumerate(((bbr, bbi), (lbr, lbi))):
        for i in range(S5_TILES // 2):
            keep = own_half & (tile_of_row == i)
            for ri, v in enumerate((vr, vi)):
                blk = jnp.where(keep, v, 0.0).astype(BF16)
                col = (2 * i + ri) * LANES
                for s in range(S5_SLABS):
                    wb_ref[s, tap * slab_rows:(tap + 1) * slab_rows, col:col + LANES] = (
                        blk[s * slab_rows:(s + 1) * slab_rows])

    og = 2 * S5_SLAB_GROUPS
    gcol = lax.broadcasted_iota(jnp.int32, (S5_STATE, og * S5_GROUP), 1) // S5_GROUP
    for n in range(S5_OCTETS):
        c_n = (crt_ref[n], -cit_ref[n])
        for sl in range(2):
            for i in range(S5_TILES // 2):
                for ri in range(2):
                    for par in range(2):
                        r0 = (((sl * (S5_TILES // 2) + i) * 2 + ri) * 2 + par) * S5_STATE
                        g_own = sl * S5_SLAB_GROUPS + 2 * i + par
                        wc_ref[n, r0:r0 + S5_STATE, :] = jnp.where(gcol == g_own, c_n[ri], 0.0).astype(BF16)


def _s5_prep(a_re, a_im, log_dt, b_re, b_im, c_re, c_im):
    rows = S5_GROUPS * S5_GROUP
    g, h, p = S5_GROUPS, S5_GROUP, S5_STATE
    og = 2 * S5_SLAB_GROUPS
    twice = lambda a: jnp.concatenate([a, a], axis=1)
    rep = lambda a: twice(jnp.repeat(a, h, axis=0))
    bt = lambda b: twice(jnp.transpose(b, (0, 2, 1)).reshape(rows, p))
    ct = lambda c: jnp.transpose(c.reshape(S5_OCTETS, og, h, p), (0, 3, 1, 2)).reshape(S5_OCTETS, p, og * h)
    ldt = jnp.broadcast_to(log_dt[:, None], (g, p))
    shp = (rows, 2 * p)
    full = lambda s: pl.BlockSpec(s, lambda: (0,) * len(s))
    cshp = (S5_OCTETS, p, og * h)
    wb_shp = (S5_SLABS, 2 * S5_SLAB_GROUPS * h, S5_SLAB_COLS)
    wc_shp = (S5_OCTETS, S5_OCTET_COLS, og * h)
    lam2r, lam2i, wb, wc = pl.pallas_call(
        _s5_prep_kernel,
        in_specs=[full(shp)] * 5 + [full(cshp)] * 2,
        out_specs=[full(shp), full(shp), full(wb_shp), full(wc_shp)],
        out_shape=[jax.ShapeDtypeStruct(shp, F32)] * 2
        + [jax.ShapeDtypeStruct(wb_shp, BF16), jax.ShapeDtypeStruct(wc_shp, BF16)],
        name="s5_prep",
    )(rep(a_re), rep(a_im), rep(ldt), bt(b_re), bt(b_im), ct(c_re), ct(c_im))

    def cols(v):
        return v[:, :p].reshape(g, h, p)[:, 0, :].reshape(S5_SLABS, S5_SLAB_GROUPS // 2, 2, p)
    lam2 = jnp.stack([cols(lam2r), cols(lam2i)], axis=2)
    lam2 = jnp.broadcast_to(lam2.reshape(1, S5_COLS), (SUBLANES, S5_COLS))
    return wb, wc, lam2


def _s5_kernel(x_ref, mod_ref, g_ref, wb_ref, wc_ref, lam2_ref, dskip_ref, wglu_ref, o_ref,
               lhs_ref, lhs2_ref, w_ref, carry_ref, y_ref, mix_ref, *, tm, nb):
    rows = nb * tm
    step = pl.program_id(0)

    @pl.when(step == 0)
    def _():
        carry_ref[...] = jnp.zeros_like(carry_ref)
        lhs2_ref[...] = jnp.zeros_like(lhs2_ref)

    @pl.when(step > 0)
    def _():
        lhs2_ref[:, 0:SUBLANES, :] = lhs2_ref[:, rows:rows + SUBLANES, :]

    g = g_ref[...]
    for b in range(nb):
        sh = mod_ref[b, 0:1, :]
        sc = mod_ref[b, 1:2, :]
        h = _rms(x_ref[b], g) * (1.0 + sc) + sh
        for s in range(S5_SLABS):
            hs = h[:, s * LANES:(s + 1) * LANES]
            lhs_ref[s, pl.ds(b, tm, stride=nb), :] = hs
            lhs2_ref[s, pl.ds(nb + b, tm, stride=nb), :] = hs

    for n in range(S5_OCTETS):
        slot = n % 2
        for sl in range(2):
            s = 2 * n + sl
            a = jnp.concatenate([lhs_ref[s], lhs2_ref[s, 0:rows, :]], axis=1).astype(BF16)
            w_ref[slot, :, sl * S5_SLAB_COLS:(sl + 1) * S5_SLAB_COLS] = jnp.dot(
                a, wb_ref[s], preferred_element_type=F32)

        base = n * S5_OCTET_COLS
        for i in range(S5_TILES):
            cr = slice(base + (2 * i) * LANES, base + (2 * i + 1) * LANES)
            ci = slice(base + (2 * i + 1) * LANES, base + (2 * i + 2) * LANES)
            lr, li = lam2_ref[:, cr], lam2_ref[:, ci]
            yr, yi = carry_ref[:, cr], carry_ref[:, ci]
            wr_c = slice((2 * i) * LANES, (2 * i + 1) * LANES)
            wi_c = slice((2 * i + 1) * LANES, (2 * i + 2) * LANES)
            for k in range(rows // SUBLANES):
                rk = slice(k * SUBLANES, (k + 1) * SUBLANES)
                nr = lr * yr - li * yi + w_ref[slot, rk, wr_c]
                ni = lr * yi + li * yr + w_ref[slot, rk, wi_c]
                w_ref[slot, rk, wr_c] = nr
                w_ref[slot, rk, wi_c] = ni
                yr, yi = nr, ni
            carry_ref[:, cr] = yr
            carry_ref[:, ci] = yi

        y_ref[:, n * 2 * LANES:(n + 1) * 2 * LANES] = jnp.dot(
            w_ref[slot].astype(BF16), wc_ref[n], preferred_element_type=F32)

    u = jnp.concatenate([lhs_ref[s] for s in range(S5_SLABS)], axis=1)
    y = y_ref[...] + dskip_ref[...] * u
    z = jax.nn.gelu(y).astype(BF16)
    gl = jnp.dot(z, wglu_ref[...], preferred_element_type=F32)
    mix = gl[:, :D_MODEL] * jax.nn.sigmoid(gl[:, D_MODEL:])
    for s in range(S5_SLABS):
        mix_ref[s] = mix[:, s * LANES:(s + 1) * LANES]
    for b in range(nb):
        yb = jnp.concatenate([mix_ref[s, pl.ds(b, tm, stride=nb), :] for s in range(S5_SLABS)], axis=1)
        o_ref[b] = x_ref[b] + mod_ref[b, 2:3, :] * yb


def _s5_layer(x, mod, g_mix, wb, wc, lam2, d_skip, w_glu):
    nb, seq, d = x.shape
    tm = S5_TM
    rows = nb * tm
    kern = functools.partial(_s5_kernel, tm=tm, nb=nb)
    return pl.pallas_call(
        kern,
        grid=(seq // tm,),
        in_specs=[
            pl.BlockSpec((nb, tm, d), lambda i: (0, i, 0)),
            _const_spec((nb, 6, d)),
            _const_spec((1, d)),
            _const_spec(wb.shape),
            _const_spec(wc.shape),
            _const_spec(lam2.shape),
            _const_spec((1, d)),
            _const_spec(w_glu.shape),
        ],
        out_specs=pl.BlockSpec((nb, tm, d), lambda i: (0, i, 0)),
        out_shape=jax.ShapeDtypeStruct(x.shape, F32),
        scratch_shapes=[
            pltpu.VMEM((S5_SLABS, rows, LANES), F32),
            pltpu.VMEM((S5_SLABS, rows + SUBLANES, LANES), F32),
            pltpu.VMEM((2, rows, S5_OCTET_COLS), F32),
            pltpu.VMEM((SUBLANES, S5_COLS), F32),
            pltpu.VMEM((rows, d), F32),
            pltpu.VMEM((S5_SLABS, rows, LANES), F32),
        ],
        compiler_params=_cparams(("arbitrary",)),
        name="s5_layer",
    )(x, mod, g_mix.reshape(1, d), wb, wc, lam2, d_skip.reshape(1, d), w_glu)


def _load_weight_bf16(src_hbm, dst_ref, stage_ref, sem, *, axis):
    chunk = stage_ref.shape[1 + axis]
    n = src_hbm.shape[axis] // chunk

    def window(ref, j):
        idx = [slice(None), slice(None)]
        idx[axis] = pl.ds(j * chunk, chunk)
        return ref.at[tuple(idx)]

    def copy(j):
        return pltpu.make_async_copy(window(src_hbm, j), stage_ref.at[j % 2], sem.at[j % 2])

    copy(0).start()
    for j in range(n):
        if j + 1 < n:
            copy(j + 1).start()
        copy(j).wait()
        window(dst_ref, j)[...] = stage_ref[j % 2].astype(BF16)


def _mlp_kernel(x_ref, mod_ref, g_ref, w1_hbm, w2_hbm, gf_ref, o_ref, w1_ref, w2_ref, *, final, layer):
    @pl.when(pl.program_id(0) == 0)
    def _():
        def load(stage_ref, sem):
            _load_weight_bf16(w1_hbm.at[layer], w1_ref, stage_ref, sem, axis=1)
            _load_weight_bf16(w2_hbm.at[layer], w2_ref, stage_ref, sem, axis=0)
        pl.run_scoped(load, pltpu.VMEM((2, D_MODEL, D_MODEL), F32), pltpu.SemaphoreType.DMA((2,)))

    x = x_ref[...]
    sh = mod_ref[0, 3:4, :]
    sc = mod_ref[0, 4:5, :]
    gt = mod_ref[0, 5:6, :]
    h = (_rms(x, g_ref[...]) * (1.0 + sc) + sh).astype(BF16)
    a = jnp.dot(h, w1_ref[...], preferred_element_type=F32)
    a = jnp.maximum(a, 0.0)
    a = (a * a).astype(BF16)
    y = jnp.dot(a, w2_ref[...], preferred_element_type=F32)
    out = x + gt * y
    if final:
        out = _rms(out, gf_ref[...])
    o_ref[...] = out


def _mlp_layer(x, mod, g_mlp, w1, w2, layer, g_final, final):
    nb, seq, d = x.shape
    tm = MLP_TM
    per_b = seq // tm
    x2 = x.reshape(nb * seq, d)
    out = pl.pallas_call(
        functools.partial(_mlp_kernel, final=final, layer=layer),
        grid=(nb * per_b,),
        in_specs=[
            pl.BlockSpec((tm, d), lambda i: (i, 0)),
            pl.BlockSpec((1, 6, d), lambda i: (i // per_b, 0, 0)),
            _const_spec((1, d)),
            pl.BlockSpec(memory_space=pl.ANY),
            pl.BlockSpec(memory_space=pl.ANY),
            _const_spec((1, d)),
        ],
        out_specs=pl.BlockSpec((tm, d), lambda i: (i, 0)),
        out_shape=jax.ShapeDtypeStruct(x2.shape, F32),
        scratch_shapes=[
            pltpu.VMEM(w1.shape[1:], BF16),
            pltpu.VMEM(w2.shape[1:], BF16),
        ],
        compiler_params=_cparams(("arbitrary",)),
        name="mlp_final" if final else "mlp_layer",
    )(x2, mod, g_mlp.reshape(1, d), w1, w2, g_final.reshape(1, d))
    return out.reshape(nb, seq, d)


def _gla_kernel(x_ref, mod_ref, g_ref, wqkv_ref, wr_ref, wglr_ref, wg2_ref, bg_ref, gn_ref, wout_ref, o_ref,
                st_ref, obuf_ref, *, tm, nb):
    @pl.when(pl.program_id(0) == 0)
    def _():
        st_ref[...] = jnp.zeros_like(st_ref)

    g = g_ref[...]
    h = jnp.concatenate(
        [(_rms(x_ref[b], g) * (1.0 + mod_ref[b, 1:2, :]) + mod_ref[b, 0:1, :]).astype(BF16) for b in range(nb)],
        axis=0)
    proj = jnp.dot(h, wqkv_ref[...], preferred_element_type=F32)
    o_q, o_k, o_v = 0, GLA_QK, 2 * GLA_QK
    q = (proj[:, o_q:o_k] * (GLA_DK ** -0.5)).astype(BF16)
    k = proj[:, o_k:o_v]
    v = proj[:, o_v:].astype(BF16)
    r = jnp.dot(h, wr_ref[...], preferred_element_type=F32)
    glr = jnp.dot(h, wglr_ref[...], preferred_element_type=F32).astype(BF16)
    pre = jnp.dot(glr, wg2_ref[...], preferred_element_type=F32) + bg_ref[...]
    la = (jnp.minimum(pre, 0.0) - jnp.log(1.0 + jnp.exp(-jnp.abs(pre)))) * (1.0 / GLA_GATE_TAU)
    la_hi = la.astype(BF16)
    la_lo = (la - la_hi.astype(F32)).astype(BF16)
    la2 = jnp.concatenate([la_hi, la_lo], axis=1)

    ti = lax.broadcasted_iota(jnp.int32, (CHUNK, CHUNK), 0)
    si = lax.broadcasted_iota(jnp.int32, (CHUNK, CHUNK), 1)
    upper = (si > ti).astype(BF16)

    nchunk = tm // CHUNK
    rows_of = lambda b, c: slice(b * tm + c * CHUNK, b * tm + (c + 1) * CHUNK)
    ks = lambda hh: slice(hh * GLA_DK, (hh + 1) * GLA_DK)
    vs = lambda hh: slice(hh * GLA_DV, (hh + 1) * GLA_DV)
    kd, dec = {}, {}
    for c in range(nchunk):
        for b in range(nb):
            rs = rows_of(b, c)
            rc2 = jnp.dot(upper, la2[rs], preferred_element_type=F32)
            rc = rc2[:, :GLA_QK] + rc2[:, GLA_QK:]
            kd[b, c] = (k[rs] * jnp.exp(rc)).astype(BF16)
            dec[b, c] = jnp.exp(jnp.sum(la[rs], axis=0, keepdims=True))

    heads = [(b, hh) for b in range(nb) for hh in range(GLA_HEADS)]
    for c in range(nchunk):
        kvt = {(b, hh): lax.dot_general(v[rows_of(b, c), vs(hh)], kd[b, c][:, ks(hh)], (((0,), (0,)), ((), ())),
                                        preferred_element_type=F32) for b, hh in heads}
        st = {}
        for b, hh in heads:
            st[b, hh] = dec[b, c][:, ks(hh)] * st_ref[b, hh] + kvt[b, hh]
            st_ref[b, hh] = st[b, hh]
        for b, hh in heads:
            obuf_ref[rows_of(b, c), vs(hh)] = lax.dot_general(
                q[rows_of(b, c), ks(hh)], st[b, hh].astype(BF16), (((1,), (1,)), ((), ())),
                preferred_element_type=F32)

    outs = []
    for hh in range(GLA_HEADS):
        oh = obuf_ref[:, vs(hh)]
        outs.append(oh * lax.rsqrt(jnp.mean(oh * oh, axis=-1, keepdims=True) + EPS))
    o = jnp.concatenate(outs, axis=1) * gn_ref[...]
    o = (o * (r * jax.nn.sigmoid(r))).astype(BF16)
    y = jnp.dot(o, wout_ref[...], preferred_element_type=F32)
    for b in range(nb):
        o_ref[b] = x_ref[b] + mod_ref[b, 2:3, :] * y[b * tm:(b + 1) * tm]


def _gla_layer(x, mod, g_mix, w_qkv, w_r, w_glr, w_g2_p, b_gate, g_norm, w_out):
    nb, seq, d = x.shape
    tm = GLA_TM
    return pl.pallas_call(
        functools.partial(_gla_kernel, tm=tm, nb=nb),
        grid=(seq // tm,),
        in_specs=[
            pl.BlockSpec((nb, tm, d), lambda i: (0, i, 0)),
            _const_spec((nb, 6, d)),
            _const_spec((1, d)),
            _const_spec(w_qkv.shape),
            _const_spec(w_r.shape),
            _const_spec(w_glr.shape),
            _const_spec(w_g2_p.shape),
            _const_spec((1, GLA_QK)),
            _const_spec((1, d)),
            _const_spec(w_out.shape),
        ],
        out_specs=pl.BlockSpec((nb, tm, d), lambda i: (0, i, 0)),
        out_shape=jax.ShapeDtypeStruct(x.shape, F32),
        scratch_shapes=[
            pltpu.VMEM((nb, GLA_HEADS, GLA_DV, GLA_DK), F32),
            pltpu.VMEM((nb * tm, d), F32),
        ],
        compiler_params=_cparams(("arbitrary",)),
        name="gla_layer",
    )(x, mod, g_mix.reshape(1, d), w_qkv, w_r, w_glr, w_g2_p, b_gate.reshape(1, GLA_QK), g_norm.reshape(1, d),
      w_out)


def kernel(x, c, w_ada, b_ada, norm_mix, norm_mlp, s5_a_re, s5_a_im, s5_log_dt, s5_b_re, s5_b_im,
           s5_c_re, s5_c_im, s5_d, s5_w_glu, gla_w_in, gla_w_gate2, gla_b_gate, gla_g_norm, gla_w_out,
           w_ff1, w_ff2, norm_final):
    bsz = x.shape[0]
    mod = _adaln(c, w_ada, b_ada).reshape(DEPTH, bsz, 6, D_MODEL)

    wb, wc, lam2 = _s5_prep(s5_a_re[0], s5_a_im[0], s5_log_dt[0], s5_b_re[0], s5_b_im[0], s5_c_re[0], s5_c_im[0])
    x = _s5_layer(x, mod[0], norm_mix[0], wb, wc, lam2, s5_d[0], s5_w_glu[0].astype(BF16))
    x = _mlp_layer(x, mod[0], norm_mlp[0], w_ff1, w_ff2, 0, norm_final, False)

    w_in = gla_w_in[0]
    o_g = 2 * GLA_QK + D_MODEL
    o_r = o_g + GLA_GATE_RANK
    w_qkv = w_in[:, :o_g].astype(BF16)
    w_r = w_in[:, o_r:].astype(BF16)
    w_glr = jnp.pad(w_in[:, o_g:o_r], ((0, 0), (0, GLA_GLR_PAD - GLA_GATE_RANK))).astype(BF16)
    w_g2_p = jnp.pad(gla_w_gate2[0], ((0, GLA_GLR_PAD - GLA_GATE_RANK), (0, 0))).astype(BF16)
    x = _gla_layer(x, mod[1], norm_mix[1], w_qkv, w_r, w_glr, w_g2_p, gla_b_gate[0], gla_g_norm[0],
                   gla_w_out[0].astype(BF16))
    x = _mlp_layer(x, mod[1], norm_mlp[1], w_ff1, w_ff2, 1, norm_final, True)
    return x
```

```python
import functools

import jax
import jax.numpy as jnp
from jax import lax
from jax.experimental import pallas as pl
from jax.experimental.pallas import tpu as pltpu

F32 = jnp.float32
BF16 = jnp.bfloat16

D_MODEL = 1024
DEPTH = 2
CHUNK = 64
S5_GROUP = 16
S5_GROUPS = D_MODEL // S5_GROUP
S5_STATE = 64
GLA_HEADS = 4
GLA_QK = D_MODEL // 2
GLA_DK = GLA_QK // GLA_HEADS
GLA_DV = D_MODEL // GLA_HEADS
GLA_GATE_RANK = 16
GLA_GATE_TAU = 16.0
EPS = 1e-6

LANES = 128
SUBLANES = 8
VMEM_LIMIT_BYTES = 56 * 1024 * 1024

S5_SLABS = D_MODEL // LANES
S5_SLAB_GROUPS = LANES // S5_GROUP
S5_SLAB_COLS = S5_SLAB_GROUPS * 2 * S5_STATE
S5_OCTETS = S5_SLABS // 2
S5_OCTET_COLS = 2 * S5_SLAB_COLS
S5_TILES = S5_OCTET_COLS // (2 * LANES)
S5_COLS = S5_SLABS * S5_SLAB_COLS

S5_TM = 128
MLP_TM = 1024
MLP_FF_CHUNK = 1024
GLA_TM = 256

GLA_GLR_PAD = LANES


def _cparams(sem):
    return pltpu.CompilerParams(dimension_semantics=sem, vmem_limit_bytes=VMEM_LIMIT_BYTES)


def _const_spec(shape):
    nd = len(shape)
    return pl.BlockSpec(shape, lambda *_: (0,) * nd, pipeline_mode=pl.Buffered(1))


def _rms(x, g):
    return x * lax.rsqrt(jnp.mean(x * x, axis=-1, keepdims=True) + EPS) * g


def _adaln_kernel(c_ref, w_ref, b_ref, o_ref):
    c = c_ref[...]
    cs = (c * jax.nn.sigmoid(c)).astype(BF16)
    o_ref[0] = jnp.dot(cs, w_ref[0].astype(BF16), preferred_element_type=F32) + b_ref[0]


def _adaln(c, w_ada, b_ada):
    nb = c.shape[0]
    bsz = -(-nb // SUBLANES) * SUBLANES
    c = jnp.pad(c, ((0, bsz - nb), (0, 0)))
    n = w_ada.shape[-1]
    tn = 1536
    out = pl.pallas_call(
        _adaln_kernel,
        grid=(DEPTH, n // tn),
        in_specs=[
            pl.BlockSpec((bsz, D_MODEL), lambda i, j: (0, 0)),
            pl.BlockSpec((1, D_MODEL, tn), lambda i, j: (i, 0, j)),
            pl.BlockSpec((1, 1, tn), lambda i, j: (i, 0, j)),
        ],
        out_specs=pl.BlockSpec((1, bsz, tn), lambda i, j: (i, 0, j)),
        out_shape=jax.ShapeDtypeStruct((DEPTH, bsz, n), F32),
        compiler_params=_cparams(("arbitrary", "arbitrary")),
        name="adaln",
    )(c, w_ada, b_ada.reshape(DEPTH, 1, n))
    return out[:, :nb]


def _s5_prep_kernel(ar_ref, ai_ref, ldt_ref, br_ref, bi_ref, crt_ref, cit_ref,
                    lam2r_ref, lam2i_ref, wb_ref, wc_ref):
    dt = jnp.exp(ldt_ref[...])
    ar = ar_ref[...]
    ai = ai_ref[...]
    mag = jnp.exp(ar * dt)
    ph = ai * dt
    lre = mag * jnp.cos(ph)
    lim = mag * jnp.sin(ph)
    den = ar * ar + ai * ai
    nr = lre - 1.0
    ni = lim
    fre = (nr * ar + ni * ai) / den
    fim = (ni * ar - nr * ai) / den
    br = br_ref[...]
    bi = bi_ref[...]
    bbr = fre * br - fim * bi
    bbi = fre * bi + fim * br
    lbr = lre * bbr - lim * bbi
    lbi = lre * bbi + lim * bbr
    lam2r_ref[...] = lre * lre - lim * lim
    lam2i_ref[...] = 2.0 * (lre * lim)

    rows = S5_GROUPS * S5_GROUP
    row = lax.broadcasted_iota(jnp.int32, (rows, LANES), 0)
    lane = lax.broadcasted_iota(jnp.int32, (rows, LANES), 1)
    grp = row // S5_GROUP
    own_half = (lane // S5_STATE) == (grp % 2)
    tile_of_row = (grp % S5_SLAB_GROUPS) // 2
    slab_rows = S5_SLAB_GROUPS * S5_GROUP
    for tap, (vr, vi) in enumerate(((bbr, bbi), (lbr, lbi))):
        for i in range(S5_TILES // 2):
            keep = own_half & (tile_of_row == i)
            for ri, v in enumerate((vr, vi)):
                blk = jnp.where(keep, v, 0.0).astype(BF16)
                col = (2 * i + ri) * LANES
                for s in range(S5_SLABS):
                    wb_ref[s, tap * slab_rows:(tap + 1) * slab_rows, col:col + LANES] = (
                        blk[s * slab_rows:(s + 1) * slab_rows])

    og = 2 * S5_SLAB_GROUPS
    gcol = lax.broadcasted_iota(jnp.int32, (S5_STATE, og * S5_GROUP), 1) // S5_GROUP
    for n in range(S5_OCTETS):
        c_n = (crt_ref[n], -cit_ref[n])
        for sl in range(2):
            for i in range(S5_TILES // 2):
                for ri in range(2):
                    for par in range(2):
                        r0 = (((sl * (S5_TILES // 2) + i) * 2 + ri) * 2 + par) * S5_STATE
                        g_own = sl * S5_SLAB_GROUPS + 2 * i + par
                        wc_ref[n, r0:r0 + S5_STATE, :] = jnp.where(gcol == g_own, c_n[ri], 0.0).astype(BF16)


def _s5_prep(a_re, a_im, log_dt, b_re, b_im, c_re, c_im):
    rows = S5_GROUPS * S5_GROUP
    g, h, p = S5_GROUPS, S5_GROUP, S5_STATE
    og = 2 * S5_SLAB_GROUPS
    twice = lambda a: jnp.concatenate([a, a], axis=1)
    rep = lambda a: twice(jnp.repeat(a, h, axis=0))
    bt = lambda b: twice(jnp.transpose(b, (0, 2, 1)).reshape(rows, p))
    ct = lambda c: jnp.transpose(c.reshape(S5_OCTETS, og, h, p), (0, 3, 1, 2)).reshape(S5_OCTETS, p, og * h)
    ldt = jnp.broadcast_to(log_dt[:, None], (g, p))
    shp = (rows, 2 * p)
    full = lambda s: pl.BlockSpec(s, lambda: (0,) * len(s))
    cshp = (S5_OCTETS, p, og * h)
    wb_shp = (S5_SLABS, 2 * S5_SLAB_GROUPS * h, S5_SLAB_COLS)
    wc_shp = (S5_OCTETS, S5_OCTET_COLS, og * h)
    lam2r, lam2i, wb, wc = pl.pallas_call(
        _s5_prep_kernel,
        in_specs=[full(shp)] * 5 + [full(cshp)] * 2,
        out_specs=[full(shp), full(shp), full(wb_shp), full(wc_shp)],
        out_shape=[jax.ShapeDtypeStruct(shp, F32)] * 2
        + [jax.ShapeDtypeStruct(wb_shp, BF16), jax.ShapeDtypeStruct(wc_shp, BF16)],
        name="s5_prep",
    )(rep(a_re), rep(a_im), rep(ldt), bt(b_re), bt(b_im), ct(c_re), ct(c_im))

    def cols(v):
        return v[:, :p].reshape(g, h, p)[:, 0, :].reshape(S5_SLABS, S5_SLAB_GROUPS // 2, 2, p)
    lam2 = jnp.stack([cols(lam2r), cols(lam2i)], axis=2)
    lam2 = jnp.broadcast_to(lam2.reshape(1, S5_COLS), (SUBLANES, S5_COLS))
    return wb, wc, lam2


def _s5_kernel(x_ref, mod_ref, g_ref, wb_ref, wc_ref, lam2_ref, dskip_ref, wglu_ref, o_ref,
               lhs_ref, lhs2_ref, w_ref, carry_ref, y_ref, mix_ref, *, tm, nb):
    rows = nb * tm
    step = pl.program_id(0)

    @pl.when(step == 0)
    def _():
        carry_ref[...] = jnp.zeros_like(carry_ref)
        lhs2_ref[...] = jnp.zeros_like(lhs2_ref)

    @pl.when(step > 0)
    def _():
        lhs2_ref[:, 0:SUBLANES, :] = lhs2_ref[:, rows:rows + SUBLANES, :]

    g = g_ref[...]
    for b in range(nb):
        sh = mod_ref[b, 0:1, :]
        sc = mod_ref[b, 1:2, :]
        h = _rms(x_ref[b], g) * (1.0 + sc) + sh
        for s in range(S5_SLABS):
            hs = h[:, s * LANES:(s + 1) * LANES]
            lhs_ref[s, pl.ds(b, tm, stride=nb), :] = hs
            lhs2_ref[s, pl.ds(nb + b, tm, stride=nb), :] = hs

    for n in range(S5_OCTETS):
        slot = n % 2
        for sl in range(2):
            s = 2 * n + sl
            a = jnp.concatenate([lhs_ref[s], lhs2_ref[s, 0:rows, :]], axis=1).astype(BF16)
            w_ref[slot, :, sl * S5_SLAB_COLS:(sl + 1) * S5_SLAB_COLS] = jnp.dot(
                a, wb_ref[s], preferred_element_type=F32)

        base = n * S5_OCTET_COLS
        for i in range(S5_TILES):
            cr = slice(base + (2 * i) * LANES, base + (2 * i + 1) * LANES)
            ci = slice(base + (2 * i + 1) * LANES, base + (2 * i + 2) * LANES)
            lr, li = lam2_ref[:, cr], lam2_ref[:, ci]
            yr, yi = carry_ref[:, cr], carry_ref[:, ci]
            wr_c = slice((2 * i) * LANES, (2 * i + 1) * LANES)
            wi_c = slice((2 * i + 1) * LANES, (2 * i + 2) * LANES)
            for k in range(rows // SUBLANES):
                rk = slice(k * SUBLANES, (k + 1) * SUBLANES)
                nr = lr * yr - li * yi + w_ref[slot, rk, wr_c]
                ni = lr * yi + li * yr + w_ref[slot, rk, wi_c]
                w_ref[slot, rk, wr_c] = nr
                w_ref[slot, rk, wi_c] = ni
                yr, yi = nr, ni
            carry_ref[:, cr] = yr
            carry_ref[:, ci] = yi

        y_ref[:, n * 2 * LANES:(n + 1) * 2 * LANES] = jnp.dot(
            w_ref[slot].astype(BF16), wc_ref[n], preferred_element_type=F32)

    u = jnp.concatenate([lhs_ref[s] for s in range(S5_SLABS)], axis=1)
    y = y_ref[...] + dskip_ref[...] * u
    z = jax.nn.gelu(y).astype(BF16)
    gl = jnp.dot(z, wglu_ref[...], preferred_element_type=F32)
    mix = gl[:, :D_MODEL] * jax.nn.sigmoid(gl[:, D_MODEL:])
    for s in range(S5_SLABS):
        mix_ref[s] = mix[:, s * LANES:(s + 1) * LANES]
    for b in range(nb):
        yb = jnp.concatenate([mix_ref[s, pl.ds(b, tm, stride=nb), :] for s in range(S5_SLABS)], axis=1)
        o_ref[b] = x_ref[b] + mod_ref[b, 2:3, :] * yb


def _s5_layer(x, mod, g_mix, wb, wc, lam2, d_skip, w_glu):
    nb, seq, d = x.shape
    tm = S5_TM
    rows = nb * tm
    kern = functools.partial(_s5_kernel, tm=tm, nb=nb)
    return pl.pallas_call(
        kern,
        grid=(seq // tm,),
        in_specs=[
            pl.BlockSpec((nb, tm, d), lambda i: (0, i, 0)),
            _const_spec((nb, 6, d)),
            _const_spec((1, d)),
            _const_spec(wb.shape),
            _const_spec(wc.shape),
            _const_spec(lam2.shape),
            _const_spec((1, d)),
            _const_spec(w_glu.shape),
        ],
        out_specs=pl.BlockSpec((nb, tm, d), lambda i: (0, i, 0)),
        out_shape=jax.ShapeDtypeStruct(x.shape, F32),
        scratch_shapes=[
            pltpu.VMEM((S5_SLABS, rows, LANES), F32),
            pltpu.VMEM((S5_SLABS, rows + SUBLANES, LANES), F32),
            pltpu.VMEM((2, rows, S5_OCTET_COLS), F32),
            pltpu.VMEM((SUBLANES, S5_COLS), F32),
            pltpu.VMEM((rows, d), F32),
            pltpu.VMEM((S5_SLABS, rows, LANES), F32),
        ],
        compiler_params=_cparams(("arbitrary",)),
        name="s5_layer",
    )(x, mod, g_mix.reshape(1, d), wb, wc, lam2, d_skip.reshape(1, d), w_glu)


def _load_weight_bf16(src_hbm, dst_ref, stage_ref, sem, *, axis):
    chunk = stage_ref.shape[1 + axis]
    n = src_hbm.shape[axis] // chunk

    def window(ref, j):
        idx = [slice(None), slice(None)]
        idx[axis] = pl.ds(j * chunk, chunk)
        return ref.at[tuple(idx)]

    def copy(j):
        return pltpu.make_async_copy(window(src_hbm, j), stage_ref.at[j % 2], sem.at[j % 2])

    copy(0).start()
    for j in range(n):
        if j + 1 < n:
            copy(j + 1).start()
        copy(j).wait()
        window(dst_ref, j)[...] = stage_ref[j % 2].astype(BF16)


def _mlp_kernel(x_ref, mod_ref, g_ref, w1_hbm, w2_hbm, gf_ref, o_ref, w1_ref, w2_ref, *, final, layer):
    @pl.when(pl.program_id(0) == 0)
    def _():
        def load(stage_ref, sem):
            _load_weight_bf16(w1_hbm.at[layer], w1_ref, stage_ref, sem, axis=1)
            _load_weight_bf16(w2_hbm.at[layer], w2_ref, stage_ref, sem, axis=0)
        pl.run_scoped(load, pltpu.VMEM((2, D_MODEL, D_MODEL), F32), pltpu.SemaphoreType.DMA((2,)))

    x = x_ref[...]
    sh = mod_ref[0, 3:4, :]
    sc = mod_ref[0, 4:5, :]
    gt = mod_ref[0, 5:6, :]
    h = (_rms(x, g_ref[...]) * (1.0 + sc) + sh).astype(BF16)
    d_ff = w1_ref.shape[1]
    y = None
    for c0 in range(0, d_ff, MLP_FF_CHUNK):
        a = jnp.dot(h, w1_ref[:, c0:c0 + MLP_FF_CHUNK], preferred_element_type=F32)
        a = jnp.maximum(a, 0.0)
        a = (a * a).astype(BF16)
        part = jnp.dot(a, w2_ref[c0:c0 + MLP_FF_CHUNK, :], preferred_element_type=F32)
        y = part if y is None else y + part
    out = x + gt * y
    if final:
        out = _rms(out, gf_ref[...])
    o_ref[...] = out


def _mlp_layer(x, mod, g_mlp, w1, w2, layer, g_final, final):
    nb, seq, d = x.shape
    tm = MLP_TM
    per_b = seq // tm
    x2 = x.reshape(nb * seq, d)
    out = pl.pallas_call(
        functools.partial(_mlp_kernel, final=final, layer=layer),
        grid=(nb * per_b,),
        in_specs=[
            pl.BlockSpec((tm, d), lambda i: (i, 0)),
            pl.BlockSpec((1, 6, d), lambda i: (i // per_b, 0, 0)),
            _const_spec((1, d)),
            pl.BlockSpec(memory_space=pl.ANY),
            pl.BlockSpec(memory_space=pl.ANY),
            _const_spec((1, d)),
        ],
        out_specs=pl.BlockSpec((tm, d), lambda i: (i, 0)),
        out_shape=jax.ShapeDtypeStruct(x2.shape, F32),
        scratch_shapes=[
            pltpu.VMEM(w1.shape[1:], BF16),
            pltpu.VMEM(w2.shape[1:], BF16),
        ],
        compiler_params=_cparams(("arbitrary",)),
        name="mlp_final" if final else "mlp_layer",
    )(x2, mod, g_mlp.reshape(1, d), w1, w2, g_final.reshape(1, d))
    return out.reshape(nb, seq, d)


def _gla_kernel(x_ref, mod_ref, g_ref, wqkv_ref, wr_ref, wglr_ref, wg2_ref, bg_ref, gn_ref, wout_ref, o_ref,
                st_ref, obuf_ref, *, tm, nb):
    @pl.when(pl.program_id(0) == 0)
    def _():
        st_ref[...] = jnp.zeros_like(st_ref)

    g = g_ref[...]
    h = jnp.concatenate(
        [(_rms(x_ref[b], g) * (1.0 + mod_ref[b, 1:2, :]) + mod_ref[b, 0:1, :]).astype(BF16) for b in range(nb)],
        axis=0)
    proj = jnp.dot(h, wqkv_ref[...], preferred_element_type=F32)
    o_q, o_k, o_v = 0, GLA_QK, 2 * GLA_QK
    q = (proj[:, o_q:o_k] * (GLA_DK ** -0.5)).astype(BF16)
    k = proj[:, o_k:o_v]
    v = proj[:, o_v:].astype(BF16)
    r = jnp.dot(h, wr_ref[...], preferred_element_type=F32)
    glr = jnp.dot(h, wglr_ref[...], preferred_element_type=F32).astype(BF16)
    pre = jnp.dot(glr, wg2_ref[...], preferred_element_type=F32) + bg_ref[...]
    la = (jnp.minimum(pre, 0.0) - jnp.log(1.0 + jnp.exp(-jnp.abs(pre)))) * (1.0 / GLA_GATE_TAU)
    la_hi = la.astype(BF16)
    la_lo = (la - la_hi.astype(F32)).astype(BF16)
    la2 = jnp.concatenate([la_hi, la_lo], axis=1)

    ti = lax.broadcasted_iota(jnp.int32, (CHUNK, CHUNK), 0)
    si = lax.broadcasted_iota(jnp.int32, (CHUNK, CHUNK), 1)
    upper = (si > ti).astype(BF16)

    nchunk = tm // CHUNK
    rows_of = lambda b, c: slice(b * tm + c * CHUNK, b * tm + (c + 1) * CHUNK)
    ks = lambda hh: slice(hh * GLA_DK, (hh + 1) * GLA_DK)
    vs = lambda hh: slice(hh * GLA_DV, (hh + 1) * GLA_DV)
    kd, dec = {}, {}
    for c in range(nchunk):
        for b in range(nb):
            rs = rows_of(b, c)
            rc2 = jnp.dot(upper, la2[rs], preferred_element_type=F32)
            rc = rc2[:, :GLA_QK] + rc2[:, GLA_QK:]
            kd[b, c] = (k[rs] * jnp.exp(rc)).astype(BF16)
            dec[b, c] = jnp.exp(jnp.sum(la[rs], axis=0, keepdims=True))

    heads = [(b, hh) for b in range(nb) for hh in range(GLA_HEADS)]
    for c in range(nchunk):
        kvt = {(b, hh): lax.dot_general(v[rows_of(b, c), vs(hh)], kd[b, c][:, ks(hh)], (((0,), (0,)), ((), ())),
                                        preferred_element_type=F32) for b, hh in heads}
        st = {}
        for b, hh in heads:
            st[b, hh] = dec[b, c][:, ks(hh)] * st_ref[b, hh] + kvt[b, hh]
            st_ref[b, hh] = st[b, hh]
        for b, hh in heads:
            obuf_ref[rows_of(b, c), vs(hh)] = lax.dot_general(
                q[rows_of(b, c), ks(hh)], st[b, hh].astype(BF16), (((1,), (1,)), ((), ())),
                preferred_element_type=F32)

    outs = []
    for hh in range(GLA_HEADS):
        oh = obuf_ref[:, vs(hh)]
        outs.append(oh * lax.rsqrt(jnp.mean(oh * oh, axis=-1, keepdims=True) + EPS))
    o = jnp.concatenate(outs, axis=1) * gn_ref[...]
    o = (o * (r * jax.nn.sigmoid(r))).astype(BF16)
    y = jnp.dot(o, wout_ref[...], preferred_element_type=F32)
    for b in range(nb):
        o_ref[b] = x_ref[b] + mod_ref[b, 2:3, :] * y[b * tm:(b + 1) * tm]


def _gla_layer(x, mod, g_mix, w_qkv, w_r, w_glr, w_g2_p, b_gate, g_norm, w_out):
    nb, seq, d = x.shape
    tm = GLA_TM
    return pl.pallas_call(
        functools.partial(_gla_kernel, tm=tm, nb=nb),
        grid=(seq // tm,),
        in_specs=[
            pl.BlockSpec((nb, tm, d), lambda i: (0, i, 0)),
            _const_spec((nb, 6, d)),
            _const_spec((1, d)),
            _const_spec(w_qkv.shape),
            _const_spec(w_r.shape),
            _const_spec(w_glr.shape),
            _const_spec(w_g2_p.shape),
            _const_spec((1, GLA_QK)),
            _const_spec((1, d)),
            _const_spec(w_out.shape),
        ],
        out_specs=pl.BlockSpec((nb, tm, d), lambda i: (0, i, 0)),
        out_shape=jax.ShapeDtypeStruct(x.shape, F32),
        scratch_shapes=[
            pltpu.VMEM((nb, GLA_HEADS, GLA_DV, GLA_DK), F32),
            pltpu.VMEM((nb * tm, d), F32),
        ],
        compiler_params=_cparams(("arbitrary",)),
        name="gla_layer",
    )(x, mod, g_mix.reshape(1, d), w_qkv, w_r, w_glr, w_g2_p, b_gate.reshape(1, GLA_QK), g_norm.reshape(1, d),
      w_out)


def kernel(x, c, w_ada, b_ada, norm_mix, norm_mlp, s5_a_re, s5_a_im, s5_log_dt, s5_b_re, s5_b_im,
           s5_c_re, s5_c_im, s5_d, s5_w_glu, gla_w_in, gla_w_gate2, gla_b_gate, gla_g_norm, gla_w_out,
           w_ff1, w_ff2, norm_final):
    bsz = x.shape[0]
    mod = _adaln(c, w_ada, b_ada).reshape(DEPTH, bsz, 6, D_MODEL)

    wb, wc, lam2 = _s5_prep(s5_a_re[0], s5_a_im[0], s5_log_dt[0], s5_b_re[0], s5_b_im[0], s5_c_re[0], s5_c_im[0])
    x = _s5_layer(x, mod[0], norm_mix[0], wb, wc, lam2, s5_d[0], s5_w_glu[0].astype(BF16))
    x = _mlp_layer(x, mod[0], norm_mlp[0], w_ff1, w_ff2, 0, norm_final, False)

    w_in = gla_w_in[0]
    o_g = 2 * GLA_QK + D_MODEL
    o_r = o_g + GLA_GATE_RANK
    w_qkv = w_in[:, :o_g].astype(BF16)
    w_r = w_in[:, o_r:].astype(BF16)
    w_glr = jnp.pad(w_in[:, o_g:o_r], ((0, 0), (0, GLA_GLR_PAD - GLA_GATE_RANK))).astype(BF16)
    w_g2_p = jnp.pad(gla_w_gate2[0], ((0, GLA_GLR_PAD - GLA_GATE_RANK), (0, 0))).astype(BF16)
    x = _gla_layer(x, mod[1], norm_mix[1], w_qkv, w_r, w_glr, w_g2_p, gla_b_gate[0], gla_g_norm[0],
                   gla_w_out[0].astype(BF16))
    x = _mlp_layer(x, mod[1], norm_mlp[1], w_ff1, w_ff2, 1, norm_final, True)
    return x
```

```python
import functools

import jax
import jax.numpy as jnp
from jax import lax
from jax.experimental import pallas as pl
from jax.experimental.pallas import tpu as pltpu

F32 = jnp.float32
BF16 = jnp.bfloat16

D_MODEL = 1024
DEPTH = 2
CHUNK = 64
S5_GROUP = 16
S5_GROUPS = D_MODEL // S5_GROUP
S5_STATE = 64
GLA_HEADS = 4
GLA_QK = D_MODEL // 2
GLA_DK = GLA_QK // GLA_HEADS
GLA_DV = D_MODEL // GLA_HEADS
GLA_GATE_RANK = 16
GLA_GATE_TAU = 16.0
EPS = 1e-6

LANES = 128
SUBLANES = 8
VMEM_LIMIT_BYTES = 56 * 1024 * 1024

S5_SLABS = D_MODEL // LANES
S5_SLAB_GROUPS = LANES // S5_GROUP
S5_SLAB_COLS = S5_SLAB_GROUPS * 2 * S5_STATE
S5_OCTETS = S5_SLABS // 2
S5_OCTET_COLS = 2 * S5_SLAB_COLS
S5_TILES = S5_OCTET_COLS // (2 * LANES)
S5_COLS = S5_SLABS * S5_SLAB_COLS

S5_TM = 128
S5_OUT_CHUNKS = 1
MLP_TM = 1024
MLP_FF_CHUNK = 1024
GLA_TM = 256

GLA_GLR_PAD = LANES


def _cparams(sem):
    return pltpu.CompilerParams(dimension_semantics=sem, vmem_limit_bytes=VMEM_LIMIT_BYTES)


def _const_spec(shape):
    nd = len(shape)
    return pl.BlockSpec(shape, lambda *_: (0,) * nd, pipeline_mode=pl.Buffered(1))


def _rms(x, g):
    return x * lax.rsqrt(jnp.mean(x * x, axis=-1, keepdims=True) + EPS) * g


def _adaln_kernel(c_ref, w_ref, b_ref, o_ref):
    c = c_ref[...]
    cs = (c * jax.nn.sigmoid(c)).astype(BF16)
    o_ref[0] = jnp.dot(cs, w_ref[0].astype(BF16), preferred_element_type=F32) + b_ref[0]


def _adaln(c, w_ada, b_ada):
    nb = c.shape[0]
    bsz = -(-nb // SUBLANES) * SUBLANES
    c = jnp.pad(c, ((0, bsz - nb), (0, 0)))
    n = w_ada.shape[-1]
    tn = 1536
    out = pl.pallas_call(
        _adaln_kernel,
        grid=(DEPTH, n // tn),
        in_specs=[
            pl.BlockSpec((bsz, D_MODEL), lambda i, j: (0, 0)),
            pl.BlockSpec((1, D_MODEL, tn), lambda i, j: (i, 0, j)),
            pl.BlockSpec((1, 1, tn), lambda i, j: (i, 0, j)),
        ],
        out_specs=pl.BlockSpec((1, bsz, tn), lambda i, j: (i, 0, j)),
        out_shape=jax.ShapeDtypeStruct((DEPTH, bsz, n), F32),
        compiler_params=_cparams(("arbitrary", "arbitrary")),
        name="adaln",
    )(c, w_ada, b_ada.reshape(DEPTH, 1, n))
    return out[:, :nb]


def _s5_prep_kernel(ar_ref, ai_ref, ldt_ref, br_ref, bi_ref, crt_ref, cit_ref,
                    lam2r_ref, lam2i_ref, wb_ref, wc_ref):
    dt = jnp.exp(ldt_ref[...])
    ar = ar_ref[...]
    ai = ai_ref[...]
    mag = jnp.exp(ar * dt)
    ph = ai * dt
    lre = mag * jnp.cos(ph)
    lim = mag * jnp.sin(ph)
    den = ar * ar + ai * ai
    nr = lre - 1.0
    ni = lim
    fre = (nr * ar + ni * ai) / den
    fim = (ni * ar - nr * ai) / den
    br = br_ref[...]
    bi = bi_ref[...]
    bbr = fre * br - fim * bi
    bbi = fre * bi + fim * br
    lbr = lre * bbr - lim * bbi
    lbi = lre * bbi + lim * bbr
    lam2r_ref[...] = lre * lre - lim * lim
    lam2i_ref[...] = 2.0 * (lre * lim)

    rows = S5_GROUPS * S5_GROUP
    row = lax.broadcasted_iota(jnp.int32, (rows, LANES), 0)
    lane = lax.broadcasted_iota(jnp.int32, (rows, LANES), 1)
    grp = row // S5_GROUP
    own_half = (lane // S5_STATE) == (grp % 2)
    tile_of_row = (grp % S5_SLAB_GROUPS) // 2
    slab_rows = S5_SLAB_GROUPS * S5_GROUP
    for tap, (vr, vi) in enumerate(((bbr, bbi), (lbr, lbi))):
        for i in range(S5_TILES // 2):
            keep = own_half & (tile_of_row == i)
            for ri, v in enumerate((vr, vi)):
                blk = jnp.where(keep, v, 0.0).astype(BF16)
                col = (2 * i + ri) * LANES
                for s in range(S5_SLABS):
                    wb_ref[s, tap * slab_rows:(tap + 1) * slab_rows, col:col + LANES] = (
                        blk[s * slab_rows:(s + 1) * slab_rows])

    og = 2 * S5_SLAB_GROUPS
    gcol = lax.broadcasted_iota(jnp.int32, (S5_STATE, og * S5_GROUP), 1) // S5_GROUP
    for n in range(S5_OCTETS):
        c_n = (crt_ref[n], -cit_ref[n])
        for sl in range(2):
            for i in range(S5_TILES // 2):
                for ri in range(2):
                    for par in range(2):
                        r0 = (((sl * (S5_TILES // 2) + i) * 2 + ri) * 2 + par) * S5_STATE
                        g_own = sl * S5_SLAB_GROUPS + 2 * i + par
                        wc_ref[n, r0:r0 + S5_STATE, :] = jnp.where(gcol == g_own, c_n[ri], 0.0).astype(BF16)


def _s5_prep(a_re, a_im, log_dt, b_re, b_im, c_re, c_im):
    rows = S5_GROUPS * S5_GROUP
    g, h, p = S5_GROUPS, S5_GROUP, S5_STATE
    og = 2 * S5_SLAB_GROUPS
    twice = lambda a: jnp.concatenate([a, a], axis=1)
    rep = lambda a: twice(jnp.repeat(a, h, axis=0))
    bt = lambda b: twice(jnp.transpose(b, (0, 2, 1)).reshape(rows, p))
    ct = lambda c: jnp.transpose(c.reshape(S5_OCTETS, og, h, p), (0, 3, 1, 2)).reshape(S5_OCTETS, p, og * h)
    ldt = jnp.broadcast_to(log_dt[:, None], (g, p))
    shp = (rows, 2 * p)
    full = lambda s: pl.BlockSpec(s, lambda: (0,) * len(s))
    cshp = (S5_OCTETS, p, og * h)
    wb_shp = (S5_SLABS, 2 * S5_SLAB_GROUPS * h, S5_SLAB_COLS)
    wc_shp = (S5_OCTETS, S5_OCTET_COLS, og * h)
    lam2r, lam2i, wb, wc = pl.pallas_call(
        _s5_prep_kernel,
        in_specs=[full(shp)] * 5 + [full(cshp)] * 2,
        out_specs=[full(shp), full(shp), full(wb_shp), full(wc_shp)],
        out_shape=[jax.ShapeDtypeStruct(shp, F32)] * 2
        + [jax.ShapeDtypeStruct(wb_shp, BF16), jax.ShapeDtypeStruct(wc_shp, BF16)],
        name="s5_prep",
    )(rep(a_re), rep(a_im), rep(ldt), bt(b_re), bt(b_im), ct(c_re), ct(c_im))

    def cols(v):
        return v[:, :p].reshape(g, h, p)[:, 0, :].reshape(S5_SLABS, S5_SLAB_GROUPS // 2, 2, p)
    lam2 = jnp.stack([cols(lam2r), cols(lam2i)], axis=2)
    lam2 = jnp.broadcast_to(lam2.reshape(1, S5_COLS), (SUBLANES, S5_COLS))
    return wb, wc, lam2


def _s5_kernel(x_ref, mod_ref, g_ref, wb_ref, wc_ref, lam2_ref, dskip_ref, wglu_hbm, o_ref,
               lhs_ref, lhs2_ref, w_ref, carry_ref, y_ref, mix_ref, wglu_ref, *, tm, nb):
    rows = nb * tm
    step = pl.program_id(0)

    @pl.when(step == 0)
    def _():
        carry_ref[...] = jnp.zeros_like(carry_ref)
        lhs2_ref[...] = jnp.zeros_like(lhs2_ref)
        pl.run_scoped(
            lambda stage_ref, sem: _load_weight_bf16(wglu_hbm, wglu_ref, stage_ref, sem, axis=1),
            pltpu.VMEM((2, D_MODEL, D_MODEL // 2), F32), pltpu.SemaphoreType.DMA((2,)))

    @pl.when(step > 0)
    def _():
        lhs2_ref[:, 0:SUBLANES, :] = lhs2_ref[:, rows:rows + SUBLANES, :]

    g = g_ref[...]
    for b in range(nb):
        sh = mod_ref[b, 0:1, :]
        sc = mod_ref[b, 1:2, :]
        h = _rms(x_ref[b], g) * (1.0 + sc) + sh
        for s in range(S5_SLABS):
            hs = h[:, s * LANES:(s + 1) * LANES]
            lhs_ref[s, pl.ds(b, tm, stride=nb), :] = hs
            lhs2_ref[s, pl.ds(nb + b, tm, stride=nb), :] = hs

    for n in range(S5_OCTETS):
        slot = n % 2
        for sl in range(2):
            s = 2 * n + sl
            a = jnp.concatenate([lhs_ref[s], lhs2_ref[s, 0:rows, :]], axis=1).astype(BF16)
            w_ref[slot, :, sl * S5_SLAB_COLS:(sl + 1) * S5_SLAB_COLS] = jnp.dot(
                a, wb_ref[s], preferred_element_type=F32)

        base = n * S5_OCTET_COLS
        for i in range(S5_TILES):
            cr = slice(base + (2 * i) * LANES, base + (2 * i + 1) * LANES)
            ci = slice(base + (2 * i + 1) * LANES, base + (2 * i + 2) * LANES)
            lr, li = lam2_ref[:, cr], lam2_ref[:, ci]
            yr, yi = carry_ref[:, cr], carry_ref[:, ci]
            wr_c = slice((2 * i) * LANES, (2 * i + 1) * LANES)
            wi_c = slice((2 * i + 1) * LANES, (2 * i + 2) * LANES)
            for k in range(rows // SUBLANES):
                rk = slice(k * SUBLANES, (k + 1) * SUBLANES)
                nr = lr * yr - li * yi + w_ref[slot, rk, wr_c]
                ni = lr * yi + li * yr + w_ref[slot, rk, wi_c]
                w_ref[slot, rk, wr_c] = nr
                w_ref[slot, rk, wi_c] = ni
                yr, yi = nr, ni
            carry_ref[:, cr] = yr
            carry_ref[:, ci] = yi

        y_ref[:, n * 2 * LANES:(n + 1) * 2 * LANES] = jnp.dot(
            w_ref[slot].astype(BF16), wc_ref[n], preferred_element_type=F32)

    tc = tm // S5_OUT_CHUNKS
    for c in range(S5_OUT_CHUNKS):
        rs = slice(c * tc * nb, (c + 1) * tc * nb)
        u = jnp.concatenate([lhs_ref[s, rs, :] for s in range(S5_SLABS)], axis=1)
        y = y_ref[rs, :] + dskip_ref[...] * u
        z = jax.nn.gelu(y).astype(BF16)
        gl = jnp.dot(z, wglu_ref[...], preferred_element_type=F32)
        mix = gl[:, :D_MODEL] * jax.nn.sigmoid(gl[:, D_MODEL:])
        for s in range(S5_SLABS):
            mix_ref[s, rs, :] = mix[:, s * LANES:(s + 1) * LANES]
        for b in range(nb):
            yb = jnp.concatenate(
                [mix_ref[s, pl.ds(c * tc * nb + b, tc, stride=nb), :] for s in range(S5_SLABS)], axis=1)
            ts = slice(c * tc, (c + 1) * tc)
            o_ref[b, ts, :] = x_ref[b, ts, :] + mod_ref[b, 2:3, :] * yb


def _s5_layer(x, mod, g_mix, wb, wc, lam2, d_skip, w_glu):
    nb, seq, d = x.shape
    tm = S5_TM
    rows = nb * tm
    kern = functools.partial(_s5_kernel, tm=tm, nb=nb)
    return pl.pallas_call(
        kern,
        grid=(seq // tm,),
        in_specs=[
            pl.BlockSpec((nb, tm, d), lambda i: (0, i, 0)),
            _const_spec((nb, 6, d)),
            _const_spec((1, d)),
            _const_spec(wb.shape),
            _const_spec(wc.shape),
            _const_spec(lam2.shape),
            _const_spec((1, d)),
            pl.BlockSpec(memory_space=pl.ANY),
        ],
        out_specs=pl.BlockSpec((nb, tm, d), lambda i: (0, i, 0)),
        out_shape=jax.ShapeDtypeStruct(x.shape, F32),
        scratch_shapes=[
            pltpu.VMEM((S5_SLABS, rows, LANES), F32),
            pltpu.VMEM((S5_SLABS, rows + SUBLANES, LANES), F32),
            pltpu.VMEM((2, rows, S5_OCTET_COLS), F32),
            pltpu.VMEM((SUBLANES, S5_COLS), F32),
            pltpu.VMEM((rows, d), F32),
            pltpu.VMEM((S5_SLABS, rows, LANES), F32),
            pltpu.VMEM(w_glu.shape, BF16),
        ],
        compiler_params=_cparams(("arbitrary",)),
        name="s5_layer",
    )(x, mod, g_mix.reshape(1, d), wb, wc, lam2, d_skip.reshape(1, d), w_glu)


def _load_weight_bf16(src_hbm, dst_ref, stage_ref, sem, *, axis):
    chunk = stage_ref.shape[1 + axis]
    n = src_hbm.shape[axis] // chunk

    def window(ref, j):
        idx = [slice(None), slice(None)]
        idx[axis] = pl.ds(j * chunk, chunk)
        return ref.at[tuple(idx)]

    def copy(j):
        return pltpu.make_async_copy(window(src_hbm, j), stage_ref.at[j % 2], sem.at[j % 2])

    copy(0).start()
    for j in range(n):
        if j + 1 < n:
            copy(j + 1).start()
        copy(j).wait()
        window(dst_ref, j)[...] = stage_ref[j % 2].astype(BF16)


def _mlp_kernel(x_ref, mod_ref, g_ref, w1_hbm, w2_hbm, gf_ref, o_ref, w1_ref, w2_ref, *, final, layer):
    @pl.when(pl.program_id(0) == 0)
    def _():
        def load(stage_ref, sem):
            _load_weight_bf16(w1_hbm.at[layer], w1_ref, stage_ref, sem, axis=1)
            _load_weight_bf16(w2_hbm.at[layer], w2_ref, stage_ref, sem, axis=0)
        pl.run_scoped(load, pltpu.VMEM((2, D_MODEL, D_MODEL), F32), pltpu.SemaphoreType.DMA((2,)))

    x = x_ref[...]
    sh = mod_ref[0, 3:4, :]
    sc = mod_ref[0, 4:5, :]
    gt = mod_ref[0, 5:6, :]
    h = (_rms(x, g_ref[...]) * (1.0 + sc) + sh).astype(BF16)
    d_ff = w1_ref.shape[1]
    y = None
    for c0 in range(0, d_ff, MLP_FF_CHUNK):
        a = jnp.dot(h, w1_ref[:, c0:c0 + MLP_FF_CHUNK], preferred_element_type=F32)
        a = jnp.maximum(a, 0.0)
        a = (a * a).astype(BF16)
        part = jnp.dot(a, w2_ref[c0:c0 + MLP_FF_CHUNK, :], preferred_element_type=F32)
        y = part if y is None else y + part
    out = x + gt * y
    if final:
        out = _rms(out, gf_ref[...])
    o_ref[...] = out


def _mlp_layer(x, mod, g_mlp, w1, w2, layer, g_final, final):
    nb, seq, d = x.shape
    tm = MLP_TM
    per_b = seq // tm
    x2 = x.reshape(nb * seq, d)
    out = pl.pallas_call(
        functools.partial(_mlp_kernel, final=final, layer=layer),
        grid=(nb * per_b,),
        in_specs=[
            pl.BlockSpec((tm, d), lambda i: (i, 0)),
            pl.BlockSpec((1, 6, d), lambda i: (i // per_b, 0, 0)),
            _const_spec((1, d)),
            pl.BlockSpec(memory_space=pl.ANY),
            pl.BlockSpec(memory_space=pl.ANY),
            _const_spec((1, d)),
        ],
        out_specs=pl.BlockSpec((tm, d), lambda i: (i, 0)),
        out_shape=jax.ShapeDtypeStruct(x2.shape, F32),
        scratch_shapes=[
            pltpu.VMEM(w1.shape[1:], BF16),
            pltpu.VMEM(w2.shape[1:], BF16),
        ],
        compiler_params=_cparams(("arbitrary",)),
        name="mlp_final" if final else "mlp_layer",
    )(x2, mod, g_mlp.reshape(1, d), w1, w2, g_final.reshape(1, d))
    return out.reshape(nb, seq, d)


def _gla_kernel(x_ref, mod_ref, g_ref, wqkv_ref, wr_ref, wglr_ref, wg2_ref, bg_ref, gn_ref, wout_ref, o_ref,
                st_ref, obuf_ref, *, tm, nb):
    @pl.when(pl.program_id(0) == 0)
    def _():
        st_ref[...] = jnp.zeros_like(st_ref)

    g = g_ref[...]
    o_q, o_k, o_v = 0, GLA_QK, 2 * GLA_QK
    q, k, v, r, la, la2 = [], [], [], [], [], []
    for b in range(nb):
        h = (_rms(x_ref[b], g) * (1.0 + mod_ref[b, 1:2, :]) + mod_ref[b, 0:1, :]).astype(BF16)
        proj = jnp.dot(h, wqkv_ref[...], preferred_element_type=F32)
        q.append((proj[:, o_q:o_k] * (GLA_DK ** -0.5)).astype(BF16))
        k.append(proj[:, o_k:o_v])
        v.append(proj[:, o_v:].astype(BF16))
        r.append(jnp.dot(h, wr_ref[...], preferred_element_type=F32))
        glr = jnp.dot(h, wglr_ref[...], preferred_element_type=F32).astype(BF16)
        pre = jnp.dot(glr, wg2_ref[...], preferred_element_type=F32) + bg_ref[...]
        la_b = (jnp.minimum(pre, 0.0) - jnp.log(1.0 + jnp.exp(-jnp.abs(pre)))) * (1.0 / GLA_GATE_TAU)
        la_hi = la_b.astype(BF16)
        la_lo = (la_b - la_hi.astype(F32)).astype(BF16)
        la.append(la_b)
        la2.append(jnp.concatenate([la_hi, la_lo], axis=1))

    ti = lax.broadcasted_iota(jnp.int32, (CHUNK, CHUNK), 0)
    si = lax.broadcasted_iota(jnp.int32, (CHUNK, CHUNK), 1)
    upper = (si > ti).astype(BF16)

    nchunk = tm // CHUNK
    rows_of = lambda c: slice(c * CHUNK, (c + 1) * CHUNK)
    ks = lambda hh: slice(hh * GLA_DK, (hh + 1) * GLA_DK)
    vs = lambda hh: slice(hh * GLA_DV, (hh + 1) * GLA_DV)
    kd, dec = {}, {}
    for c in range(nchunk):
        for b in range(nb):
            rs = rows_of(c)
            rc2 = jnp.dot(upper, la2[b][rs], preferred_element_type=F32)
            rc = rc2[:, :GLA_QK] + rc2[:, GLA_QK:]
            kd[b, c] = (k[b][rs] * jnp.exp(rc)).astype(BF16)
            dec[b, c] = jnp.exp(jnp.sum(la[b][rs], axis=0, keepdims=True))

    heads = [(b, hh) for b in range(nb) for hh in range(GLA_HEADS)]
    for c in range(nchunk):
        rs = rows_of(c)
        kvt = {(b, hh): lax.dot_general(v[b][rs, vs(hh)], kd[b, c][:, ks(hh)], (((0,), (0,)), ((), ())),
                                        preferred_element_type=F32) for b, hh in heads}
        st = {}
        for b, hh in heads:
            st[b, hh] = dec[b, c][:, ks(hh)] * st_ref[b, hh] + kvt[b, hh]
            st_ref[b, hh] = st[b, hh]
        for b, hh in heads:
            obuf_ref[b, rs, vs(hh)] = lax.dot_general(
                q[b][rs, ks(hh)], st[b, hh].astype(BF16), (((1,), (1,)), ((), ())),
                preferred_element_type=F32)

    for b in range(nb):
        outs = []
        for hh in range(GLA_HEADS):
            oh = obuf_ref[b, :, vs(hh)]
            outs.append(oh * lax.rsqrt(jnp.mean(oh * oh, axis=-1, keepdims=True) + EPS))
        o = jnp.concatenate(outs, axis=1) * gn_ref[...]
        o = (o * (r[b] * jax.nn.sigmoid(r[b]))).astype(BF16)
        y = jnp.dot(o, wout_ref[...], preferred_element_type=F32)
        o_ref[b] = x_ref[b] + mod_ref[b, 2:3, :] * y


def _gla_layer(x, mod, g_mix, w_qkv, w_r, w_glr, w_g2_p, b_gate, g_norm, w_out):
    nb, seq, d = x.shape
    tm = GLA_TM
    return pl.pallas_call(
        functools.partial(_gla_kernel, tm=tm, nb=nb),
        grid=(seq // tm,),
        in_specs=[
            pl.BlockSpec((nb, tm, d), lambda i: (0, i, 0)),
            _const_spec((nb, 6, d)),
            _const_spec((1, d)),
            _const_spec(w_qkv.shape),
            _const_spec(w_r.shape),
            _const_spec(w_glr.shape),
            _const_spec(w_g2_p.shape),
            _const_spec((1, GLA_QK)),
            _const_spec((1, d)),
            _const_spec(w_out.shape),
        ],
        out_specs=pl.BlockSpec((nb, tm, d), lambda i: (0, i, 0)),
        out_shape=jax.ShapeDtypeStruct(x.shape, F32),
        scratch_shapes=[
            pltpu.VMEM((nb, GLA_HEADS, GLA_DV, GLA_DK), F32),
            pltpu.VMEM((nb, tm, d), F32),
        ],
        compiler_params=_cparams(("arbitrary",)),
        name="gla_layer",
    )(x, mod, g_mix.reshape(1, d), w_qkv, w_r, w_glr, w_g2_p, b_gate.reshape(1, GLA_QK), g_norm.reshape(1, d),
      w_out)


def kernel(x, c, w_ada, b_ada, norm_mix, norm_mlp, s5_a_re, s5_a_im, s5_log_dt, s5_b_re, s5_b_im,
           s5_c_re, s5_c_im, s5_d, s5_w_glu, gla_w_in, gla_w_gate2, gla_b_gate, gla_g_norm, gla_w_out,
           w_ff1, w_ff2, norm_final):
    bsz = x.shape[0]
    mod = _adaln(c, w_ada, b_ada).reshape(DEPTH, bsz, 6, D_MODEL)

    wb, wc, lam2 = _s5_prep(s5_a_re[0], s5_a_im[0], s5_log_dt[0], s5_b_re[0], s5_b_im[0], s5_c_re[0], s5_c_im[0])
    x = _s5_layer(x, mod[0], norm_mix[0], wb, wc, lam2, s5_d[0], s5_w_glu[0])
    x = _mlp_layer(x, mod[0], norm_mlp[0], w_ff1, w_ff2, 0, norm_final, False)

    w_in = gla_w_in[0]
    o_g = 2 * GLA_QK + D_MODEL
    o_r = o_g + GLA_GATE_RANK
    w_qkv = w_in[:, :o_g].astype(BF16)
    w_r = w_in[:, o_r:].astype(BF16)
    w_glr = jnp.pad(w_in[:, o_g:o_r], ((0, 0), (0, GLA_GLR_PAD - GLA_GATE_RANK))).astype(BF16)
    w_g2_p = jnp.pad(gla_w_gate2[0], ((0, GLA_GLR_PAD - GLA_GATE_RANK), (0, 0))).astype(BF16)
    x = _gla_layer(x, mod[1], norm_mix[1], w_qkv, w_r, w_glr, w_g2_p, gla_b_gate[0], gla_g_norm[0],
                   gla_w_out[0].astype(BF16))
    x = _mlp_layer(x, mod[1], norm_mlp[1], w_ff1, w_ff2, 1, norm_final, True)
    return x
```

```python
import functools

import jax
import jax.numpy as jnp
from jax import lax
from jax.experimental import pallas as pl
from jax.experimental.pallas import tpu as pltpu

F32 = jnp.float32
BF16 = jnp.bfloat16

D_MODEL = 1024
DEPTH = 2
CHUNK = 64
S5_GROUP = 16
S5_GROUPS = D_MODEL // S5_GROUP
S5_STATE = 64
GLA_HEADS = 4
GLA_QK = D_MODEL // 2
GLA_DK = GLA_QK // GLA_HEADS
GLA_DV = D_MODEL // GLA_HEADS
GLA_GATE_RANK = 16
GLA_GATE_TAU = 16.0
EPS = 1e-6

LANES = 128
SUBLANES = 8
VMEM_LIMIT_BYTES = 56 * 1024 * 1024

S5_SLABS = D_MODEL // LANES
S5_SLAB_GROUPS = LANES // S5_GROUP
S5_SLAB_COLS = S5_SLAB_GROUPS * 2 * S5_STATE
S5_OCTETS = S5_SLABS // 2
S5_OCTET_COLS = 2 * S5_SLAB_COLS
S5_TILES = S5_OCTET_COLS // (2 * LANES)
S5_COLS = S5_SLABS * S5_SLAB_COLS

S5_TM = 128
ADALN_TN = 1536
MLP_TM = 1024
MLP_FF_CHUNK = 1024
GLA_TM = 256

GLA_GLR_PAD = LANES


def _cparams(sem):
    return pltpu.CompilerParams(dimension_semantics=sem, vmem_limit_bytes=VMEM_LIMIT_BYTES)


def _const_spec(shape):
    nd = len(shape)
    return pl.BlockSpec(shape, lambda *_: (0,) * nd, pipeline_mode=pl.Buffered(1))


def _rms(x, g):
    return x * lax.rsqrt(jnp.mean(x * x, axis=-1, keepdims=True) + EPS) * g


def _adaln_kernel(c_ref, w_ref, b_ref, o_ref):
    c = c_ref[...]
    cs = (c * jax.nn.sigmoid(c)).astype(BF16)
    o_ref[0] = jnp.dot(cs, w_ref[0].astype(BF16), preferred_element_type=F32) + b_ref[0]


def _adaln(c, w_ada, b_ada):
    nb = c.shape[0]
    bsz = -(-nb // SUBLANES) * SUBLANES
    c = jnp.pad(c, ((0, bsz - nb), (0, 0)))
    n = w_ada.shape[-1]
    tn = ADALN_TN
    out = pl.pallas_call(
        _adaln_kernel,
        grid=(DEPTH, n // tn),
        in_specs=[
            pl.BlockSpec((bsz, D_MODEL), lambda i, j: (0, 0)),
            pl.BlockSpec((1, D_MODEL, tn), lambda i, j: (i, 0, j)),
            pl.BlockSpec((1, 1, tn), lambda i, j: (i, 0, j)),
        ],
        out_specs=pl.BlockSpec((1, bsz, tn), lambda i, j: (i, 0, j)),
        out_shape=jax.ShapeDtypeStruct((DEPTH, bsz, n), F32),
        compiler_params=_cparams(("arbitrary", "arbitrary")),
        name="adaln",
    )(c, w_ada, b_ada.reshape(DEPTH, 1, n))
    return out[:, :nb]


def _s5_prep_kernel(ar_ref, ai_ref, ldt_ref, br_ref, bi_ref, crt_ref, cit_ref,
                    lam2r_ref, lam2i_ref, wb_ref, wc_ref):
    dt = jnp.exp(ldt_ref[...])
    ar = ar_ref[...]
    ai = ai_ref[...]
    mag = jnp.exp(ar * dt)
    ph = ai * dt
    lre = mag * jnp.cos(ph)
    lim = mag * jnp.sin(ph)
    den = ar * ar + ai * ai
    nr = lre - 1.0
    ni = lim
    fre = (nr * ar + ni * ai) / den
    fim = (ni * ar - nr * ai) / den
    br = br_ref[...]
    bi = bi_ref[...]
    bbr = fre * br - fim * bi
    bbi = fre * bi + fim * br
    lbr = lre * bbr - lim * bbi
    lbi = lre * bbi + lim * bbr
    lam2r_ref[...] = lre * lre - lim * lim
    lam2i_ref[...] = 2.0 * (lre * lim)

    rows = S5_GROUPS * S5_GROUP
    row = lax.broadcasted_iota(jnp.int32, (rows, LANES), 0)
    lane = lax.broadcasted_iota(jnp.int32, (rows, LANES), 1)
    grp = row // S5_GROUP
    own_half = (lane // S5_STATE) == (grp % 2)
    tile_of_row = (grp % S5_SLAB_GROUPS) // 2
    slab_rows = S5_SLAB_GROUPS * S5_GROUP
    for tap, (vr, vi) in enumerate(((bbr, bbi), (lbr, lbi))):
        for i in range(S5_TILES // 2):
            keep = own_half & (tile_of_row == i)
            for ri, v in enumerate((vr, vi)):
                blk = jnp.where(keep, v, 0.0).astype(BF16)
                col = (2 * i + ri) * LANES
                for s in range(S5_SLABS):
                    wb_ref[s, tap * slab_rows:(tap + 1) * slab_rows, col:col + LANES] = (
                        blk[s * slab_rows:(s + 1) * slab_rows])

    og = 2 * S5_SLAB_GROUPS
    gcol = lax.broadcasted_iota(jnp.int32, (S5_STATE, og * S5_GROUP), 1) // S5_GROUP
    for n in range(S5_OCTETS):
        c_n = (crt_ref[n], -cit_ref[n])
        for sl in range(2):
            for i in range(S5_TILES // 2):
                for ri in range(2):
                    for par in range(2):
                        r0 = (((sl * (S5_TILES // 2) + i) * 2 + ri) * 2 + par) * S5_STATE
                        g_own = sl * S5_SLAB_GROUPS + 2 * i + par
                        wc_ref[n, r0:r0 + S5_STATE, :] = jnp.where(gcol == g_own, c_n[ri], 0.0).astype(BF16)


def _s5_prep(a_re, a_im, log_dt, b_re, b_im, c_re, c_im):
    rows = S5_GROUPS * S5_GROUP
    g, h, p = S5_GROUPS, S5_GROUP, S5_STATE
    og = 2 * S5_SLAB_GROUPS
    twice = lambda a: jnp.concatenate([a, a], axis=1)
    rep = lambda a: twice(jnp.repeat(a, h, axis=0))
    bt = lambda b: twice(jnp.transpose(b, (0, 2, 1)).reshape(rows, p))
    ct = lambda c: jnp.transpose(c.reshape(S5_OCTETS, og, h, p), (0, 3, 1, 2)).reshape(S5_OCTETS, p, og * h)
    ldt = jnp.broadcast_to(log_dt[:, None], (g, p))
    shp = (rows, 2 * p)
    full = lambda s: pl.BlockSpec(s, lambda: (0,) * len(s))
    cshp = (S5_OCTETS, p, og * h)
    wb_shp = (S5_SLABS, 2 * S5_SLAB_GROUPS * h, S5_SLAB_COLS)
    wc_shp = (S5_OCTETS, S5_OCTET_COLS, og * h)
    lam2r, lam2i, wb, wc = pl.pallas_call(
        _s5_prep_kernel,
        in_specs=[full(shp)] * 5 + [full(cshp)] * 2,
        out_specs=[full(shp), full(shp), full(wb_shp), full(wc_shp)],
        out_shape=[jax.ShapeDtypeStruct(shp, F32)] * 2
        + [jax.ShapeDtypeStruct(wb_shp, BF16), jax.ShapeDtypeStruct(wc_shp, BF16)],
        name="s5_prep",
    )(rep(a_re), rep(a_im), rep(ldt), bt(b_re), bt(b_im), ct(c_re), ct(c_im))

    def cols(v):
        return v[:, :p].reshape(g, h, p)[:, 0, :].reshape(S5_SLABS, S5_SLAB_GROUPS // 2, 2, p)
    lam2 = jnp.stack([cols(lam2r), cols(lam2i)], axis=2)
    lam2 = jnp.broadcast_to(lam2.reshape(1, S5_COLS), (SUBLANES, S5_COLS))
    return wb, wc, lam2


def _s5_kernel(x_ref, mod_ref, g_ref, wb_ref, wc_ref, lam2_ref, dskip_ref, wglu_hbm, o_ref,
               lhs_ref, lhs2_ref, w_ref, carry_ref, y_ref, mix_ref, wglu_ref, *, tm, nb):
    rows = nb * tm
    step = pl.program_id(0)

    @pl.when(step == 0)
    def _():
        carry_ref[...] = jnp.zeros_like(carry_ref)
        lhs2_ref[...] = jnp.zeros_like(lhs2_ref)
        pl.run_scoped(
            lambda stage_ref, sem: _load_weight_bf16(wglu_hbm, wglu_ref, stage_ref, sem, axis=1),
            pltpu.VMEM((2, D_MODEL, D_MODEL // 2), F32), pltpu.SemaphoreType.DMA((2,)))

    @pl.when(step > 0)
    def _():
        lhs2_ref[:, 0:SUBLANES, :] = lhs2_ref[:, rows:rows + SUBLANES, :]

    g = g_ref[...]
    for b in range(nb):
        sh = mod_ref[b, 0:1, :]
        sc = mod_ref[b, 1:2, :]
        h = _rms(x_ref[b], g) * (1.0 + sc) + sh
        for s in range(S5_SLABS):
            hs = h[:, s * LANES:(s + 1) * LANES]
            lhs_ref[s, pl.ds(b, tm, stride=nb), :] = hs
            lhs2_ref[s, pl.ds(nb + b, tm, stride=nb), :] = hs

    for n in range(S5_OCTETS):
        slot = n % 2
        for sl in range(2):
            s = 2 * n + sl
            a = jnp.concatenate([lhs_ref[s], lhs2_ref[s, 0:rows, :]], axis=1).astype(BF16)
            w_ref[slot, :, sl * S5_SLAB_COLS:(sl + 1) * S5_SLAB_COLS] = jnp.dot(
                a, wb_ref[s], preferred_element_type=F32)

        base = n * S5_OCTET_COLS
        for i in range(S5_TILES):
            cr = slice(base + (2 * i) * LANES, base + (2 * i + 1) * LANES)
            ci = slice(base + (2 * i + 1) * LANES, base + (2 * i + 2) * LANES)
            lr, li = lam2_ref[:, cr], lam2_ref[:, ci]
            yr, yi = carry_ref[:, cr], carry_ref[:, ci]
            wr_c = slice((2 * i) * LANES, (2 * i + 1) * LANES)
            wi_c = slice((2 * i + 1) * LANES, (2 * i + 2) * LANES)
            for k in range(rows // SUBLANES):
                rk = slice(k * SUBLANES, (k + 1) * SUBLANES)
                nr = lr * yr - li * yi + w_ref[slot, rk, wr_c]
                ni = lr * yi + li * yr + w_ref[slot, rk, wi_c]
                w_ref[slot, rk, wr_c] = nr
                w_ref[slot, rk, wi_c] = ni
                yr, yi = nr, ni
            carry_ref[:, cr] = yr
            carry_ref[:, ci] = yi

        for hr in range(2):
            rh = slice(hr * rows // 2, (hr + 1) * rows // 2)
            y_ref[rh, n * 2 * LANES:(n + 1) * 2 * LANES] = jnp.dot(
                w_ref[slot, rh, :].astype(BF16), wc_ref[n], preferred_element_type=F32)

    u = jnp.concatenate([lhs_ref[s] for s in range(S5_SLABS)], axis=1)
    y = y_ref[...] + dskip_ref[...] * u
    z = jax.nn.gelu(y).astype(BF16)
    gl = jnp.dot(z, wglu_ref[...], preferred_element_type=F32)
    mix = gl[:, :D_MODEL] * jax.nn.sigmoid(gl[:, D_MODEL:])
    for s in range(S5_SLABS):
        mix_ref[s] = mix[:, s * LANES:(s + 1) * LANES]
    for b in range(nb):
        yb = jnp.concatenate([mix_ref[s, pl.ds(b, tm, stride=nb), :] for s in range(S5_SLABS)], axis=1)
        o_ref[b] = x_ref[b] + mod_ref[b, 2:3, :] * yb


def _s5_layer(x, mod, g_mix, wb, wc, lam2, d_skip, w_glu):
    nb, seq, d = x.shape
    tm = S5_TM
    rows = nb * tm
    kern = functools.partial(_s5_kernel, tm=tm, nb=nb)
    return pl.pallas_call(
        kern,
        grid=(seq // tm,),
        in_specs=[
            pl.BlockSpec((nb, tm, d), lambda i: (0, i, 0)),
            _const_spec((nb, 6, d)),
            _const_spec((1, d)),
            _const_spec(wb.shape),
            _const_spec(wc.shape),
            _const_spec(lam2.shape),
            _const_spec((1, d)),
            pl.BlockSpec(memory_space=pl.ANY),
        ],
        out_specs=pl.BlockSpec((nb, tm, d), lambda i: (0, i, 0)),
        out_shape=jax.ShapeDtypeStruct(x.shape, F32),
        scratch_shapes=[
            pltpu.VMEM((S5_SLABS, rows, LANES), F32),
            pltpu.VMEM((S5_SLABS, rows + SUBLANES, LANES), F32),
            pltpu.VMEM((2, rows, S5_OCTET_COLS), F32),
            pltpu.VMEM((SUBLANES, S5_COLS), F32),
            pltpu.VMEM((rows, d), F32),
            pltpu.VMEM((S5_SLABS, rows, LANES), F32),
            pltpu.VMEM(w_glu.shape, BF16),
        ],
        compiler_params=_cparams(("arbitrary",)),
        name="s5_layer",
    )(x, mod, g_mix.reshape(1, d), wb, wc, lam2, d_skip.reshape(1, d), w_glu)


def _load_weight_bf16(src_hbm, dst_ref, stage_ref, sem, *, axis):
    chunk = stage_ref.shape[1 + axis]
    n = src_hbm.shape[axis] // chunk

    def window(ref, j):
        idx = [slice(None), slice(None)]
        idx[axis] = pl.ds(j * chunk, chunk)
        return ref.at[tuple(idx)]

    def copy(j):
        return pltpu.make_async_copy(window(src_hbm, j), stage_ref.at[j % 2], sem.at[j % 2])

    copy(0).start()
    for j in range(n):
        if j + 1 < n:
            copy(j + 1).start()
        copy(j).wait()
        window(dst_ref, j)[...] = stage_ref[j % 2].astype(BF16)


def _mlp_kernel(x_ref, mod_ref, g_ref, w1_hbm, w2_hbm, gf_ref, o_ref, w1_ref, w2_ref, *, final, layer):
    @pl.when(pl.program_id(0) == 0)
    def _():
        def load(stage_ref, sem):
            _load_weight_bf16(w1_hbm.at[layer], w1_ref, stage_ref, sem, axis=1)
            _load_weight_bf16(w2_hbm.at[layer], w2_ref, stage_ref, sem, axis=0)
        pl.run_scoped(load, pltpu.VMEM((2, D_MODEL, D_MODEL), F32), pltpu.SemaphoreType.DMA((2,)))

    x = x_ref[...]
    sh = mod_ref[0, 3:4, :]
    sc = mod_ref[0, 4:5, :]
    gt = mod_ref[0, 5:6, :]
    h = (_rms(x, g_ref[...]) * (1.0 + sc) + sh).astype(BF16)
    d_ff = w1_ref.shape[1]
    y = None
    for c0 in range(0, d_ff, MLP_FF_CHUNK):
        a = jnp.dot(h, w1_ref[:, c0:c0 + MLP_FF_CHUNK], preferred_element_type=F32)
        a = jnp.maximum(a, 0.0)
        a = (a * a).astype(BF16)
        part = jnp.dot(a, w2_ref[c0:c0 + MLP_FF_CHUNK, :], preferred_element_type=F32)
        y = part if y is None else y + part
    out = x + gt * y
    if final:
        out = _rms(out, gf_ref[...])
    o_ref[...] = out


def _mlp_layer(x, mod, g_mlp, w1, w2, layer, g_final, final):
    nb, seq, d = x.shape
    tm = MLP_TM
    per_b = seq // tm
    x2 = x.reshape(nb * seq, d)
    out = pl.pallas_call(
        functools.partial(_mlp_kernel, final=final, layer=layer),
        grid=(nb * per_b,),
        in_specs=[
            pl.BlockSpec((tm, d), lambda i: (i, 0)),
            pl.BlockSpec((1, 6, d), lambda i: (i // per_b, 0, 0)),
            _const_spec((1, d)),
            pl.BlockSpec(memory_space=pl.ANY),
            pl.BlockSpec(memory_space=pl.ANY),
            _const_spec((1, d)),
        ],
        out_specs=pl.BlockSpec((tm, d), lambda i: (i, 0)),
        out_shape=jax.ShapeDtypeStruct(x2.shape, F32),
        scratch_shapes=[
            pltpu.VMEM(w1.shape[1:], BF16),
            pltpu.VMEM(w2.shape[1:], BF16),
        ],
        compiler_params=_cparams(("arbitrary",)),
        name="mlp_final" if final else "mlp_layer",
    )(x2, mod, g_mlp.reshape(1, d), w1, w2, g_final.reshape(1, d))
    return out.reshape(nb, seq, d)


def _gla_kernel(x_ref, mod_ref, g_ref, wqkv_ref, wr_ref, wglr_ref, wg2_ref, bg_ref, gn_ref, wout_ref, o_ref,
                st_ref, obuf_ref, *, tm, nb):
    @pl.when(pl.program_id(0) == 0)
    def _():
        st_ref[...] = jnp.zeros_like(st_ref)

    g = g_ref[...]
    o_q, o_k, o_v = 0, GLA_QK, 2 * GLA_QK
    q, k, v, r, la, la2 = [], [], [], [], [], []
    for b in range(nb):
        h = (_rms(x_ref[b], g) * (1.0 + mod_ref[b, 1:2, :]) + mod_ref[b, 0:1, :]).astype(BF16)
        proj = jnp.dot(h, wqkv_ref[...], preferred_element_type=F32)
        q.append((proj[:, o_q:o_k] * (GLA_DK ** -0.5)).astype(BF16))
        k.append(proj[:, o_k:o_v])
        v.append(proj[:, o_v:].astype(BF16))
        r.append(jnp.dot(h, wr_ref[...], preferred_element_type=F32))
        glr = jnp.dot(h, wglr_ref[...], preferred_element_type=F32).astype(BF16)
        pre = jnp.dot(glr, wg2_ref[...], preferred_element_type=F32) + bg_ref[...]
        la_b = (jnp.minimum(pre, 0.0) - jnp.log(1.0 + jnp.exp(-jnp.abs(pre)))) * (1.0 / GLA_GATE_TAU)
        la_hi = la_b.astype(BF16)
        la_lo = (la_b - la_hi.astype(F32)).astype(BF16)
        la.append(la_b)
        la2.append(jnp.concatenate([la_hi, la_lo], axis=1))

    ti = lax.broadcasted_iota(jnp.int32, (CHUNK, CHUNK), 0)
    si = lax.broadcasted_iota(jnp.int32, (CHUNK, CHUNK), 1)
    upper = (si > ti).astype(BF16)

    nchunk = tm // CHUNK
    rows_of = lambda c: slice(c * CHUNK, (c + 1) * CHUNK)
    ks = lambda hh: slice(hh * GLA_DK, (hh + 1) * GLA_DK)
    vs = lambda hh: slice(hh * GLA_DV, (hh + 1) * GLA_DV)
    kd, dec = {}, {}
    for c in range(nchunk):
        for b in range(nb):
            rs = rows_of(c)
            rc2 = jnp.dot(upper, la2[b][rs], preferred_element_type=F32)
            rc = rc2[:, :GLA_QK] + rc2[:, GLA_QK:]
            kd[b, c] = (k[b][rs] * jnp.exp(rc)).astype(BF16)
            dec[b, c] = jnp.exp(jnp.sum(la[b][rs], axis=0, keepdims=True))

    heads = [(b, hh) for b in range(nb) for hh in range(GLA_HEADS)]
    for c in range(nchunk):
        rs = rows_of(c)
        kvt = {(b, hh): lax.dot_general(v[b][rs, vs(hh)], kd[b, c][:, ks(hh)], (((0,), (0,)), ((), ())),
                                        preferred_element_type=F32) for b, hh in heads}
        st = {}
        for b, hh in heads:
            st[b, hh] = dec[b, c][:, ks(hh)] * st_ref[b, hh] + kvt[b, hh]
            st_ref[b, hh] = st[b, hh]
        for b, hh in heads:
            obuf_ref[b, rs, vs(hh)] = lax.dot_general(
                q[b][rs, ks(hh)], st[b, hh].astype(BF16), (((1,), (1,)), ((), ())),
                preferred_element_type=F32)

    for b in range(nb):
        outs = []
        for hh in range(GLA_HEADS):
            oh = obuf_ref[b, :, vs(hh)]
            outs.append(oh * lax.rsqrt(jnp.mean(oh * oh, axis=-1, keepdims=True) + EPS))
        o = jnp.concatenate(outs, axis=1) * gn_ref[...]
        o = (o * (r[b] * jax.nn.sigmoid(r[b]))).astype(BF16)
        y = jnp.dot(o, wout_ref[...], preferred_element_type=F32)
        o_ref[b] = x_ref[b] + mod_ref[b, 2:3, :] * y


def _gla_layer(x, mod, g_mix, w_qkv, w_r, w_glr, w_g2_p, b_gate, g_norm, w_out):
    nb, seq, d = x.shape
    tm = GLA_TM
    return pl.pallas_call(
        functools.partial(_gla_kernel, tm=tm, nb=nb),
        grid=(seq // tm,),
        in_specs=[
            pl.BlockSpec((nb, tm, d), lambda i: (0, i, 0)),
            _const_spec((nb, 6, d)),
            _const_spec((1, d)),
            _const_spec(w_qkv.shape),
            _const_spec(w_r.shape),
            _const_spec(w_glr.shape),
            _const_spec(w_g2_p.shape),
            _const_spec((1, GLA_QK)),
            _const_spec((1, d)),
            _const_spec(w_out.shape),
        ],
        out_specs=pl.BlockSpec((nb, tm, d), lambda i: (0, i, 0)),
        out_shape=jax.ShapeDtypeStruct(x.shape, F32),
        scratch_shapes=[
            pltpu.VMEM((nb, GLA_HEADS, GLA_DV, GLA_DK), F32),
            pltpu.VMEM((nb, tm, d), F32),
        ],
        compiler_params=_cparams(("arbitrary",)),
        name="gla_layer",
    )(x, mod, g_mix.reshape(1, d), w_qkv, w_r, w_glr, w_g2_p, b_gate.reshape(1, GLA_QK), g_norm.reshape(1, d),
      w_out)


def kernel(x, c, w_ada, b_ada, norm_mix, norm_mlp, s5_a_re, s5_a_im, s5_log_dt, s5_b_re, s5_b_im,
           s5_c_re, s5_c_im, s5_d, s5_w_glu, gla_w_in, gla_w_gate2, gla_b_gate, gla_g_norm, gla_w_out,
           w_ff1, w_ff2, norm_final):
    bsz = x.shape[0]
    mod = _adaln(c, w_ada, b_ada).reshape(DEPTH, bsz, 6, D_MODEL)

    wb, wc, lam2 = _s5_prep(s5_a_re[0], s5_a_im[0], s5_log_dt[0], s5_b_re[0], s5_b_im[0], s5_c_re[0], s5_c_im[0])
    x = _s5_layer(x, mod[0], norm_mix[0], wb, wc, lam2, s5_d[0], s5_w_glu[0])
    x = _mlp_layer(x, mod[0], norm_mlp[0], w_ff1, w_ff2, 0, norm_final, False)

    w_in = gla_w_in[0]
    o_g = 2 * GLA_QK + D_MODEL
    o_r = o_g + GLA_GATE_RANK
    w_qkv = w_in[:, :o_g].astype(BF16)
    w_r = w_in[:, o_r:].astype(BF16)
    w_glr = jnp.pad(w_in[:, o_g:o_r], ((0, 0), (0, GLA_GLR_PAD - GLA_GATE_RANK))).astype(BF16)
    w_g2_p = jnp.pad(gla_w_gate2[0], ((0, GLA_GLR_PAD - GLA_GATE_RANK), (0, 0))).astype(BF16)
    x = _gla_layer(x, mod[1], norm_mix[1], w_qkv, w_r, w_glr, w_g2_p, gla_b_gate[0], gla_g_norm[0],
                   gla_w_out[0].astype(BF16))
    x = _mlp_layer(x, mod[1], norm_mlp[1], w_ff1, w_ff2, 1, norm_final, True)
    return x
```

```python
import functools

import jax
import jax.numpy as jnp
from jax import lax
from jax.experimental import pallas as pl
from jax.experimental.pallas import tpu as pltpu

F32 = jnp.float32
BF16 = jnp.bfloat16

D_MODEL = 1024
DEPTH = 2
CHUNK = 64
S5_GROUP = 16
S5_GROUPS = D_MODEL // S5_GROUP
S5_STATE = 64
GLA_HEADS = 4
GLA_QK = D_MODEL // 2
GLA_DK = GLA_QK // GLA_HEADS
GLA_DV = D_MODEL // GLA_HEADS
GLA_GATE_RANK = 16
GLA_GATE_TAU = 16.0
EPS = 1e-6

LANES = 128
SUBLANES = 8
VMEM_LIMIT_BYTES = 56 * 1024 * 1024

S5_SLABS = D_MODEL // LANES
S5_SLAB_GROUPS = LANES // S5_GROUP
S5_SLAB_COLS = S5_SLAB_GROUPS * 2 * S5_STATE
S5_OCTETS = S5_SLABS // 2
S5_OCTET_COLS = 2 * S5_SLAB_COLS
S5_TILES = S5_OCTET_COLS // (2 * LANES)
S5_COLS = S5_SLABS * S5_SLAB_COLS

S5_TM = 128
ADALN_TN = 1536
MLP_TM = 1024
MLP_FF_CHUNK = 1024
GLA_TM = 256

GLA_GLR_PAD = LANES


def _cparams(sem):
    return pltpu.CompilerParams(dimension_semantics=sem, vmem_limit_bytes=VMEM_LIMIT_BYTES)


def _const_spec(shape):
    nd = len(shape)
    return pl.BlockSpec(shape, lambda *_: (0,) * nd, pipeline_mode=pl.Buffered(1))


def _rms(x, g):
    return x * lax.rsqrt(jnp.mean(x * x, axis=-1, keepdims=True) + EPS) * g


def _adaln_kernel(c_ref, w_ref, b_ref, o_ref):
    c = c_ref[...]
    cs = (c * jax.nn.sigmoid(c)).astype(BF16)
    o_ref[0] = jnp.dot(cs, w_ref[0].astype(BF16), preferred_element_type=F32) + b_ref[0]


def _adaln(c, w_ada, b_ada):
    nb = c.shape[0]
    bsz = -(-nb // SUBLANES) * SUBLANES
    c = jnp.pad(c, ((0, bsz - nb), (0, 0)))
    n = w_ada.shape[-1]
    tn = ADALN_TN
    out = pl.pallas_call(
        _adaln_kernel,
        grid=(DEPTH, n // tn),
        in_specs=[
            pl.BlockSpec((bsz, D_MODEL), lambda i, j: (0, 0)),
            pl.BlockSpec((1, D_MODEL, tn), lambda i, j: (i, 0, j)),
            pl.BlockSpec((1, 1, tn), lambda i, j: (i, 0, j)),
        ],
        out_specs=pl.BlockSpec((1, bsz, tn), lambda i, j: (i, 0, j)),
        out_shape=jax.ShapeDtypeStruct((DEPTH, bsz, n), F32),
        compiler_params=_cparams(("arbitrary", "arbitrary")),
        name="adaln",
    )(c, w_ada, b_ada.reshape(DEPTH, 1, n))
    return out[:, :nb]


def _s5_prep_kernel(ar_ref, ai_ref, ldt_ref, br_ref, bi_ref, crt_ref, cit_ref,
                    lam2r_ref, lam2i_ref, wb_ref, wc_ref):
    dt = jnp.exp(ldt_ref[...])
    ar = ar_ref[...]
    ai = ai_ref[...]
    mag = jnp.exp(ar * dt)
    ph = ai * dt
    lre = mag * jnp.cos(ph)
    lim = mag * jnp.sin(ph)
    den = ar * ar + ai * ai
    nr = lre - 1.0
    ni = lim
    fre = (nr * ar + ni * ai) / den
    fim = (ni * ar - nr * ai) / den
    br = br_ref[...]
    bi = bi_ref[...]
    bbr = fre * br - fim * bi
    bbi = fre * bi + fim * br
    lbr = lre * bbr - lim * bbi
    lbi = lre * bbi + lim * bbr
    lam2r_ref[...] = lre * lre - lim * lim
    lam2i_ref[...] = 2.0 * (lre * lim)

    rows = S5_GROUPS * S5_GROUP
    row = lax.broadcasted_iota(jnp.int32, (rows, LANES), 0)
    lane = lax.broadcasted_iota(jnp.int32, (rows, LANES), 1)
    grp = row // S5_GROUP
    own_half = (lane // S5_STATE) == (grp % 2)
    tile_of_row = (grp % S5_SLAB_GROUPS) // 2
    slab_rows = S5_SLAB_GROUPS * S5_GROUP
    for tap, (vr, vi) in enumerate(((bbr, bbi), (lbr, lbi))):
        for i in range(S5_TILES // 2):
            keep = own_half & (tile_of_row == i)
            for ri, v in enumerate((vr, vi)):
                blk = jnp.where(keep, v, 0.0).astype(BF16)
                col = (2 * i + ri) * LANES
                for s in range(S5_SLABS):
                    wb_ref[s, tap * slab_rows:(tap + 1) * slab_rows, col:col + LANES] = (
                        blk[s * slab_rows:(s + 1) * slab_rows])

    og = 2 * S5_SLAB_GROUPS
    gcol = lax.broadcasted_iota(jnp.int32, (S5_STATE, og * S5_GROUP), 1) // S5_GROUP
    for n in range(S5_OCTETS):
        c_n = (crt_ref[n], -cit_ref[n])
        for sl in range(2):
            for i in range(S5_TILES // 2):
                for ri in range(2):
                    for par in range(2):
                        r0 = (((sl * (S5_TILES // 2) + i) * 2 + ri) * 2 + par) * S5_STATE
                        g_own = sl * S5_SLAB_GROUPS + 2 * i + par
                        wc_ref[n, r0:r0 + S5_STATE, :] = jnp.where(gcol == g_own, c_n[ri], 0.0).astype(BF16)


def _s5_prep(a_re, a_im, log_dt, b_re, b_im, c_re, c_im):
    rows = S5_GROUPS * S5_GROUP
    g, h, p = S5_GROUPS, S5_GROUP, S5_STATE
    og = 2 * S5_SLAB_GROUPS
    twice = lambda a: jnp.concatenate([a, a], axis=1)
    rep = lambda a: twice(jnp.repeat(a, h, axis=0))
    bt = lambda b: twice(jnp.transpose(b, (0, 2, 1)).reshape(rows, p))
    ct = lambda c: jnp.transpose(c.reshape(S5_OCTETS, og, h, p), (0, 3, 1, 2)).reshape(S5_OCTETS, p, og * h)
    ldt = jnp.broadcast_to(log_dt[:, None], (g, p))
    shp = (rows, 2 * p)
    full = lambda s: pl.BlockSpec(s, lambda: (0,) * len(s))
    cshp = (S5_OCTETS, p, og * h)
    wb_shp = (S5_SLABS, 2 * S5_SLAB_GROUPS * h, S5_SLAB_COLS)
    wc_shp = (S5_OCTETS, S5_OCTET_COLS, og * h)
    lam2r, lam2i, wb, wc = pl.pallas_call(
        _s5_prep_kernel,
        in_specs=[full(shp)] * 5 + [full(cshp)] * 2,
        out_specs=[full(shp), full(shp), full(wb_shp), full(wc_shp)],
        out_shape=[jax.ShapeDtypeStruct(shp, F32)] * 2
        + [jax.ShapeDtypeStruct(wb_shp, BF16), jax.ShapeDtypeStruct(wc_shp, BF16)],
        name="s5_prep",
    )(rep(a_re), rep(a_im), rep(ldt), bt(b_re), bt(b_im), ct(c_re), ct(c_im))

    def cols(v):
        return v[:, :p].reshape(g, h, p)[:, 0, :].reshape(S5_SLABS, S5_SLAB_GROUPS // 2, 2, p)
    lam2 = jnp.stack([cols(lam2r), cols(lam2i)], axis=2)
    lam2 = jnp.broadcast_to(lam2.reshape(1, S5_COLS), (SUBLANES, S5_COLS))
    return wb, wc, lam2


def _s5_kernel(x_ref, mod_ref, g_ref, wb_ref, wc_ref, lam2_ref, dskip_ref, wglu_hbm, o_ref,
               lhs_ref, lhs2_ref, w_ref, carry_ref, y_ref, mix_ref, wglu_ref, *, tm, nb):
    rows = nb * tm
    step = pl.program_id(0)

    @pl.when(step == 0)
    def _():
        carry_ref[...] = jnp.zeros_like(carry_ref)
        lhs2_ref[...] = jnp.zeros_like(lhs2_ref)
        pl.run_scoped(
            lambda stage_ref, sem: _load_weight_bf16(wglu_hbm, wglu_ref, stage_ref, sem, axis=1),
            pltpu.VMEM((2, D_MODEL, D_MODEL // 2), F32), pltpu.SemaphoreType.DMA((2,)))

    @pl.when(step > 0)
    def _():
        lhs2_ref[:, 0:SUBLANES, :] = lhs2_ref[:, rows:rows + SUBLANES, :]

    g = g_ref[...]
    for b in range(nb):
        sh = mod_ref[b, 0:1, :]
        sc = mod_ref[b, 1:2, :]
        h = _rms(x_ref[b], g) * (1.0 + sc) + sh
        for s in range(S5_SLABS):
            hs = h[:, s * LANES:(s + 1) * LANES]
            lhs_ref[s, pl.ds(b, tm, stride=nb), :] = hs
            lhs2_ref[s, pl.ds(nb + b, tm, stride=nb), :] = hs

    for n in range(S5_OCTETS):
        slot = n % 2
        for sl in range(2):
            s = 2 * n + sl
            a = jnp.concatenate([lhs_ref[s], lhs2_ref[s, 0:rows, :]], axis=1).astype(BF16)
            w_ref[slot, :, sl * S5_SLAB_COLS:(sl + 1) * S5_SLAB_COLS] = jnp.dot(
                a, wb_ref[s], preferred_element_type=F32)

        base = n * S5_OCTET_COLS
        for i in range(S5_TILES):
            cr = slice(base + (2 * i) * LANES, base + (2 * i + 1) * LANES)
            ci = slice(base + (2 * i + 1) * LANES, base + (2 * i + 2) * LANES)
            lr, li = lam2_ref[:, cr], lam2_ref[:, ci]
            yr, yi = carry_ref[:, cr], carry_ref[:, ci]
            wr_c = slice((2 * i) * LANES, (2 * i + 1) * LANES)
            wi_c = slice((2 * i + 1) * LANES, (2 * i + 2) * LANES)
            for k in range(rows // SUBLANES):
                rk = slice(k * SUBLANES, (k + 1) * SUBLANES)
                nr = lr * yr - li * yi + w_ref[slot, rk, wr_c]
                ni = lr * yi + li * yr + w_ref[slot, rk, wi_c]
                w_ref[slot, rk, wr_c] = nr
                w_ref[slot, rk, wi_c] = ni
                yr, yi = nr, ni
            carry_ref[:, cr] = yr
            carry_ref[:, ci] = yi

        y_ref[:, n * 2 * LANES:(n + 1) * 2 * LANES] = jnp.dot(
            w_ref[slot].astype(BF16), wc_ref[n], preferred_element_type=F32)

    u = jnp.concatenate([lhs_ref[s] for s in range(S5_SLABS)], axis=1)
    y = y_ref[...] + dskip_ref[...] * u
    z = jax.nn.gelu(y).astype(BF16)
    gl = jnp.dot(z, wglu_ref[...], preferred_element_type=F32)
    mix = gl[:, :D_MODEL] * jax.nn.sigmoid(gl[:, D_MODEL:])
    for s in range(S5_SLABS):
        mix_ref[s] = mix[:, s * LANES:(s + 1) * LANES]
    for b in range(nb):
        yb = jnp.concatenate([mix_ref[s, pl.ds(b, tm, stride=nb), :] for s in range(S5_SLABS)], axis=1)
        o_ref[b] = x_ref[b] + mod_ref[b, 2:3, :] * yb


def _s5_layer(x, mod, g_mix, wb, wc, lam2, d_skip, w_glu):
    nb, seq, d = x.shape
    tm = S5_TM
    rows = nb * tm
    kern = functools.partial(_s5_kernel, tm=tm, nb=nb)
    return pl.pallas_call(
        kern,
        grid=(seq // tm,),
        in_specs=[
            pl.BlockSpec((nb, tm, d), lambda i: (0, i, 0)),
            _const_spec((nb, 6, d)),
            _const_spec((1, d)),
            _const_spec(wb.shape),
            _const_spec(wc.shape),
            _const_spec(lam2.shape),
            _const_spec((1, d)),
            pl.BlockSpec(memory_space=pl.ANY),
        ],
        out_specs=pl.BlockSpec((nb, tm, d), lambda i: (0, i, 0)),
        out_shape=jax.ShapeDtypeStruct(x.shape, F32),
        scratch_shapes=[
            pltpu.VMEM((S5_SLABS, rows, LANES), F32),
            pltpu.VMEM((S5_SLABS, rows + SUBLANES, LANES), F32),
            pltpu.VMEM((2, rows, S5_OCTET_COLS), F32),
            pltpu.VMEM((SUBLANES, S5_COLS), F32),
            pltpu.VMEM((rows, d), F32),
            pltpu.VMEM((S5_SLABS, rows, LANES), F32),
            pltpu.VMEM(w_glu.shape, BF16),
        ],
        compiler_params=_cparams(("arbitrary",)),
        name="s5_layer",
    )(x, mod, g_mix.reshape(1, d), wb, wc, lam2, d_skip.reshape(1, d), w_glu)


def _load_weight_bf16(src_hbm, dst_ref, stage_ref, sem, *, axis):
    chunk = stage_ref.shape[1 + axis]
    n = src_hbm.shape[axis] // chunk

    def window(ref, j):
        idx = [slice(None), slice(None)]
        idx[axis] = pl.ds(j * chunk, chunk)
        return ref.at[tuple(idx)]

    def copy(j):
        return pltpu.make_async_copy(window(src_hbm, j), stage_ref.at[j % 2], sem.at[j % 2])

    copy(0).start()
    for j in range(n):
        if j + 1 < n:
            copy(j + 1).start()
        copy(j).wait()
        window(dst_ref, j)[...] = stage_ref[j % 2].astype(BF16)


def _mlp_kernel(x_ref, mod_ref, g_ref, w1_hbm, w2_hbm, gf_ref, o_ref, w1_ref, w2_ref, *, final, layer):
    @pl.when(pl.program_id(0) == 0)
    def _():
        def load(stage_ref, sem):
            _load_weight_bf16(w1_hbm.at[layer], w1_ref, stage_ref, sem, axis=1)
            _load_weight_bf16(w2_hbm.at[layer], w2_ref, stage_ref, sem, axis=0)
        pl.run_scoped(load, pltpu.VMEM((2, D_MODEL, D_MODEL), F32), pltpu.SemaphoreType.DMA((2,)))

    x = x_ref[...]
    sh = mod_ref[0, 3:4, :]
    sc = mod_ref[0, 4:5, :]
    gt = mod_ref[0, 5:6, :]
    h = (_rms(x, g_ref[...]) * (1.0 + sc) + sh).astype(BF16)
    d_ff = w1_ref.shape[1]
    y = None
    for c0 in range(0, d_ff, MLP_FF_CHUNK):
        a = jnp.dot(h, w1_ref[:, c0:c0 + MLP_FF_CHUNK], preferred_element_type=F32)
        a = jnp.maximum(a, 0.0)
        a = (a * a).astype(BF16)
        part = jnp.dot(a, w2_ref[c0:c0 + MLP_FF_CHUNK, :], preferred_element_type=F32)
        y = part if y is None else y + part
    out = x + gt * y
    if final:
        out = _rms(out, gf_ref[...])
    o_ref[...] = out


def _mlp_layer(x, mod, g_mlp, w1, w2, layer, g_final, final):
    nb, seq, d = x.shape
    tm = MLP_TM
    per_b = seq // tm
    x2 = x.reshape(nb * seq, d)
    out = pl.pallas_call(
        functools.partial(_mlp_kernel, final=final, layer=layer),
        grid=(nb * per_b,),
        in_specs=[
            pl.BlockSpec((tm, d), lambda i: (i, 0)),
            pl.BlockSpec((1, 6, d), lambda i: (i // per_b, 0, 0)),
            _const_spec((1, d)),
            pl.BlockSpec(memory_space=pl.ANY),
            pl.BlockSpec(memory_space=pl.ANY),
            _const_spec((1, d)),
        ],
        out_specs=pl.BlockSpec((tm, d), lambda i: (i, 0)),
        out_shape=jax.ShapeDtypeStruct(x2.shape, F32),
        scratch_shapes=[
            pltpu.VMEM(w1.shape[1:], BF16),
            pltpu.VMEM(w2.shape[1:], BF16),
        ],
        compiler_params=_cparams(("arbitrary",)),
        name="mlp_final" if final else "mlp_layer",
    )(x2, mod, g_mlp.reshape(1, d), w1, w2, g_final.reshape(1, d))
    return out.reshape(nb, seq, d)


def _gla_kernel(x_ref, mod_ref, g_ref, win_hbm, wr_hbm, wglr_ref, wg2_ref, bg_ref, gn_ref, wout_hbm, o_ref,
                st_ref, obuf_ref, wqkv_ref, wr_ref, wout_ref, *, tm, nb):
    @pl.when(pl.program_id(0) == 0)
    def _():
        st_ref[...] = jnp.zeros_like(st_ref)

        def load(stage_ref, sem):
            _load_weight_bf16(win_hbm.at[:, pl.ds(0, wqkv_ref.shape[1])], wqkv_ref, stage_ref, sem, axis=1)
            _load_weight_bf16(wr_hbm, wr_ref, stage_ref, sem, axis=1)
            _load_weight_bf16(wout_hbm, wout_ref, stage_ref, sem, axis=1)
        pl.run_scoped(load, pltpu.VMEM((2, D_MODEL, D_MODEL), F32), pltpu.SemaphoreType.DMA((2,)))

    g = g_ref[...]
    o_q, o_k, o_v = 0, GLA_QK, 2 * GLA_QK
    q, k, v, r, la, la2 = [], [], [], [], [], []
    for b in range(nb):
        h = (_rms(x_ref[b], g) * (1.0 + mod_ref[b, 1:2, :]) + mod_ref[b, 0:1, :]).astype(BF16)
        proj = jnp.dot(h, wqkv_ref[...], preferred_element_type=F32)
        q.append((proj[:, o_q:o_k] * (GLA_DK ** -0.5)).astype(BF16))
        k.append(proj[:, o_k:o_v])
        v.append(proj[:, o_v:].astype(BF16))
        r.append(jnp.dot(h, wr_ref[...], preferred_element_type=F32))
        glr = jnp.dot(h, wglr_ref[...], preferred_element_type=F32).astype(BF16)
        pre = jnp.dot(glr, wg2_ref[...], preferred_element_type=F32) + bg_ref[...]
        la_b = (jnp.minimum(pre, 0.0) - jnp.log(1.0 + jnp.exp(-jnp.abs(pre)))) * (1.0 / GLA_GATE_TAU)
        la_hi = la_b.astype(BF16)
        la_lo = (la_b - la_hi.astype(F32)).astype(BF16)
        la.append(la_b)
        la2.append(jnp.concatenate([la_hi, la_lo], axis=1))

    ti = lax.broadcasted_iota(jnp.int32, (CHUNK, CHUNK), 0)
    si = lax.broadcasted_iota(jnp.int32, (CHUNK, CHUNK), 1)
    upper = (si > ti).astype(BF16)

    nchunk = tm // CHUNK
    rows_of = lambda c: slice(c * CHUNK, (c + 1) * CHUNK)
    ks = lambda hh: slice(hh * GLA_DK, (hh + 1) * GLA_DK)
    vs = lambda hh: slice(hh * GLA_DV, (hh + 1) * GLA_DV)
    kd, dec = {}, {}
    for c in range(nchunk):
        for b in range(nb):
            rs = rows_of(c)
            rc2 = jnp.dot(upper, la2[b][rs], preferred_element_type=F32)
            rc = rc2[:, :GLA_QK] + rc2[:, GLA_QK:]
            kd[b, c] = (k[b][rs] * jnp.exp(rc)).astype(BF16)
            dec[b, c] = jnp.exp(jnp.sum(la[b][rs], axis=0, keepdims=True))

    heads = [(b, hh) for b in range(nb) for hh in range(GLA_HEADS)]
    for c in range(nchunk):
        rs = rows_of(c)
        kvt = {(b, hh): lax.dot_general(v[b][rs, vs(hh)], kd[b, c][:, ks(hh)], (((0,), (0,)), ((), ())),
                                        preferred_element_type=F32) for b, hh in heads}
        st = {}
        for b, hh in heads:
            st[b, hh] = dec[b, c][:, ks(hh)] * st_ref[b, hh] + kvt[b, hh]
            st_ref[b, hh] = st[b, hh]
        for b, hh in heads:
            obuf_ref[b, rs, vs(hh)] = lax.dot_general(
                q[b][rs, ks(hh)], st[b, hh].astype(BF16), (((1,), (1,)), ((), ())),
                preferred_element_type=F32)

    for b in range(nb):
        outs = []
        for hh in range(GLA_HEADS):
            oh = obuf_ref[b, :, vs(hh)]
            outs.append(oh * lax.rsqrt(jnp.mean(oh * oh, axis=-1, keepdims=True) + EPS))
        o = jnp.concatenate(outs, axis=1) * gn_ref[...]
        o = (o * (r[b] * jax.nn.sigmoid(r[b]))).astype(BF16)
        y = jnp.dot(o, wout_ref[...], preferred_element_type=F32)
        o_ref[b] = x_ref[b] + mod_ref[b, 2:3, :] * y


def _gla_layer(x, mod, g_mix, w_in, w_r, w_glr, w_g2_p, b_gate, g_norm, w_out):
    nb, seq, d = x.shape
    tm = GLA_TM
    hbm = pl.BlockSpec(memory_space=pl.ANY)
    return pl.pallas_call(
        functools.partial(_gla_kernel, tm=tm, nb=nb),
        grid=(seq // tm,),
        in_specs=[
            pl.BlockSpec((nb, tm, d), lambda i: (0, i, 0)),
            _const_spec((nb, 6, d)),
            _const_spec((1, d)),
            hbm,
            hbm,
            _const_spec(w_glr.shape),
            _const_spec(w_g2_p.shape),
            _const_spec((1, GLA_QK)),
            _const_spec((1, d)),
            hbm,
        ],
        out_specs=pl.BlockSpec((nb, tm, d), lambda i: (0, i, 0)),
        out_shape=jax.ShapeDtypeStruct(x.shape, F32),
        scratch_shapes=[
            pltpu.VMEM((nb, GLA_HEADS, GLA_DV, GLA_DK), F32),
            pltpu.VMEM((nb, tm, d), F32),
            pltpu.VMEM((d, 2 * GLA_QK + d), BF16),
            pltpu.VMEM((d, d), BF16),
            pltpu.VMEM((d, d), BF16),
        ],
        compiler_params=_cparams(("arbitrary",)),
        name="gla_layer",
    )(x, mod, g_mix.reshape(1, d), w_in, w_r, w_glr, w_g2_p, b_gate.reshape(1, GLA_QK), g_norm.reshape(1, d),
      w_out)


def kernel(x, c, w_ada, b_ada, norm_mix, norm_mlp, s5_a_re, s5_a_im, s5_log_dt, s5_b_re, s5_b_im,
           s5_c_re, s5_c_im, s5_d, s5_w_glu, gla_w_in, gla_w_gate2, gla_b_gate, gla_g_norm, gla_w_out,
           w_ff1, w_ff2, norm_final):
    bsz = x.shape[0]
    mod = _adaln(c, w_ada, b_ada).reshape(DEPTH, bsz, 6, D_MODEL)

    wb, wc, lam2 = _s5_prep(s5_a_re[0], s5_a_im[0], s5_log_dt[0], s5_b_re[0], s5_b_im[0], s5_c_re[0], s5_c_im[0])
    x = _s5_layer(x, mod[0], norm_mix[0], wb, wc, lam2, s5_d[0], s5_w_glu[0])
    x = _mlp_layer(x, mod[0], norm_mlp[0], w_ff1, w_ff2, 0, norm_final, False)

    w_in = gla_w_in[0]
    o_g = 2 * GLA_QK + D_MODEL
    o_r = o_g + GLA_GATE_RANK
    w_r = w_in[:, o_r:]
    w_glr = jnp.pad(w_in[:, o_g:o_r], ((0, 0), (0, GLA_GLR_PAD - GLA_GATE_RANK))).astype(BF16)
    w_g2_p = jnp.pad(gla_w_gate2[0], ((0, GLA_GLR_PAD - GLA_GATE_RANK), (0, 0))).astype(BF16)
    x = _gla_layer(x, mod[1], norm_mix[1], w_in, w_r, w_glr, w_g2_p, gla_b_gate[0], gla_g_norm[0], gla_w_out[0])
    x = _mlp_layer(x, mod[1], norm_mlp[1], w_ff1, w_ff2, 1, norm_final, True)
    return x
```

```python
import functools

import jax
import jax.numpy as jnp
from jax import lax
from jax.experimental import pallas as pl
from jax.experimental.pallas import tpu as pltpu

F32 = jnp.float32
BF16 = jnp.bfloat16

D_MODEL = 1024
DEPTH = 2
CHUNK = 64
S5_GROUP = 16
S5_GROUPS = D_MODEL // S5_GROUP
S5_STATE = 64
GLA_HEADS = 4
GLA_QK = D_MODEL // 2
GLA_DK = GLA_QK // GLA_HEADS
GLA_DV = D_MODEL // GLA_HEADS
GLA_GATE_RANK = 16
GLA_GATE_TAU = 16.0
EPS = 1e-6

LANES = 128
SUBLANES = 8
VMEM_LIMIT_BYTES = 56 * 1024 * 1024

S5_SLABS = D_MODEL // LANES
S5_SLAB_GROUPS = LANES // S5_GROUP
S5_SLAB_COLS = S5_SLAB_GROUPS * 2 * S5_STATE
S5_OCTETS = S5_SLABS // 2
S5_OCTET_COLS = 2 * S5_SLAB_COLS
S5_TILES = S5_OCTET_COLS // (2 * LANES)
S5_COLS = S5_SLABS * S5_SLAB_COLS

S5_TM = 128
ADALN_TN = 3072
MLP_TM = 1024
MLP_FF_CHUNK = 1024
GLA_TM = 256

GLA_GLR_PAD = LANES


def _cparams(sem):
    return pltpu.CompilerParams(dimension_semantics=sem, vmem_limit_bytes=VMEM_LIMIT_BYTES)


def _const_spec(shape):
    nd = len(shape)
    return pl.BlockSpec(shape, lambda *_: (0,) * nd, pipeline_mode=pl.Buffered(1))


def _rms(x, g):
    return x * lax.rsqrt(jnp.mean(x * x, axis=-1, keepdims=True) + EPS) * g


def _adaln_kernel(c_ref, w_ref, b_ref, o_ref):
    c = c_ref[...]
    cs = (c * jax.nn.sigmoid(c)).astype(BF16)
    o_ref[0] = jnp.dot(cs, w_ref[0].astype(BF16), preferred_element_type=F32) + b_ref[0]


def _adaln(c, w_ada, b_ada):
    nb = c.shape[0]
    bsz = -(-nb // SUBLANES) * SUBLANES
    c = jnp.pad(c, ((0, bsz - nb), (0, 0)))
    n = w_ada.shape[-1]
    tn = ADALN_TN
    out = pl.pallas_call(
        _adaln_kernel,
        grid=(DEPTH, n // tn),
        in_specs=[
            pl.BlockSpec((bsz, D_MODEL), lambda i, j: (0, 0)),
            pl.BlockSpec((1, D_MODEL, tn), lambda i, j: (i, 0, j)),
            pl.BlockSpec((1, 1, tn), lambda i, j: (i, 0, j)),
        ],
        out_specs=pl.BlockSpec((1, bsz, tn), lambda i, j: (i, 0, j)),
        out_shape=jax.ShapeDtypeStruct((DEPTH, bsz, n), F32),
        compiler_params=_cparams(("arbitrary", "arbitrary")),
        name="adaln",
    )(c, w_ada, b_ada.reshape(DEPTH, 1, n))
    return out[:, :nb]


def _s5_prep_kernel(ar_ref, ai_ref, ldt_ref, br_ref, bi_ref, crt_ref, cit_ref,
                    lam2r_ref, lam2i_ref, wb_ref, wc_ref):
    dt = jnp.exp(ldt_ref[...])
    ar = ar_ref[...]
    ai = ai_ref[...]
    mag = jnp.exp(ar * dt)
    ph = ai * dt
    lre = mag * jnp.cos(ph)
    lim = mag * jnp.sin(ph)
    den = ar * ar + ai * ai
    nr = lre - 1.0
    ni = lim
    fre = (nr * ar + ni * ai) / den
    fim = (ni * ar - nr * ai) / den
    br = br_ref[...]
    bi = bi_ref[...]
    bbr = fre * br - fim * bi
    bbi = fre * bi + fim * br
    lbr = lre * bbr - lim * bbi
    lbi = lre * bbi + lim * bbr
    lam2r_ref[...] = lre * lre - lim * lim
    lam2i_ref[...] = 2.0 * (lre * lim)

    rows = S5_GROUPS * S5_GROUP
    row = lax.broadcasted_iota(jnp.int32, (rows, LANES), 0)
    lane = lax.broadcasted_iota(jnp.int32, (rows, LANES), 1)
    grp = row // S5_GROUP
    own_half = (lane // S5_STATE) == (grp % 2)
    tile_of_row = (grp % S5_SLAB_GROUPS) // 2
    slab_rows = S5_SLAB_GROUPS * S5_GROUP
    for tap, (vr, vi) in enumerate(((bbr, bbi), (lbr, lbi))):
        for i in range(S5_TILES // 2):
            keep = own_half & (tile_of_row == i)
            for ri, v in enumerate((vr, vi)):
                blk = jnp.where(keep, v, 0.0).astype(BF16)
                col = (2 * i + ri) * LANES
                for s in range(S5_SLABS):
                    wb_ref[s, tap * slab_rows:(tap + 1) * slab_rows, col:col + LANES] = (
                        blk[s * slab_rows:(s + 1) * slab_rows])

    og = 2 * S5_SLAB_GROUPS
    gcol = lax.broadcasted_iota(jnp.int32, (S5_STATE, og * S5_GROUP), 1) // S5_GROUP
    for n in range(S5_OCTETS):
        c_n = (crt_ref[n], -cit_ref[n])
        for sl in range(2):
            for i in range(S5_TILES // 2):
                for ri in range(2):
                    for par in range(2):
                        r0 = (((sl * (S5_TILES // 2) + i) * 2 + ri) * 2 + par) * S5_STATE
                        g_own = sl * S5_SLAB_GROUPS + 2 * i + par
                        wc_ref[n, r0:r0 + S5_STATE, :] = jnp.where(gcol == g_own, c_n[ri], 0.0).astype(BF16)


def _s5_prep(a_re, a_im, log_dt, b_re, b_im, c_re, c_im):
    rows = S5_GROUPS * S5_GROUP
    g, h, p = S5_GROUPS, S5_GROUP, S5_STATE
    og = 2 * S5_SLAB_GROUPS
    twice = lambda a: jnp.concatenate([a, a], axis=1)
    rep = lambda a: twice(jnp.repeat(a, h, axis=0))
    bt = lambda b: twice(jnp.transpose(b, (0, 2, 1)).reshape(rows, p))
    ct = lambda c: jnp.transpose(c.reshape(S5_OCTETS, og, h, p), (0, 3, 1, 2)).reshape(S5_OCTETS, p, og * h)
    ldt = jnp.broadcast_to(log_dt[:, None], (g, p))
    shp = (rows, 2 * p)
    full = lambda s: pl.BlockSpec(s, lambda: (0,) * len(s))
    cshp = (S5_OCTETS, p, og * h)
    wb_shp = (S5_SLABS, 2 * S5_SLAB_GROUPS * h, S5_SLAB_COLS)
    wc_shp = (S5_OCTETS, S5_OCTET_COLS, og * h)
    lam2r, lam2i, wb, wc = pl.pallas_call(
        _s5_prep_kernel,
        in_specs=[full(shp)] * 5 + [full(cshp)] * 2,
        out_specs=[full(shp), full(shp), full(wb_shp), full(wc_shp)],
        out_shape=[jax.ShapeDtypeStruct(shp, F32)] * 2
        + [jax.ShapeDtypeStruct(wb_shp, BF16), jax.ShapeDtypeStruct(wc_shp, BF16)],
        name="s5_prep",
    )(rep(a_re), rep(a_im), rep(ldt), bt(b_re), bt(b_im), ct(c_re), ct(c_im))

    def cols(v):
        return v[:, :p].reshape(g, h, p)[:, 0, :].reshape(S5_SLABS, S5_SLAB_GROUPS // 2, 2, p)
    lam2 = jnp.stack([cols(lam2r), cols(lam2i)], axis=2)
    lam2 = jnp.broadcast_to(lam2.reshape(1, S5_COLS), (SUBLANES, S5_COLS))
    return wb, wc, lam2


def _s5_kernel(x_ref, mod_ref, g_ref, wb_ref, wc_ref, lam2_ref, dskip_ref, wglu_hbm, o_ref,
               lhs_ref, lhs2_ref, w_ref, carry_ref, y_ref, mix_ref, wglu_ref, *, tm, nb):
    rows = nb * tm
    step = pl.program_id(0)

    @pl.when(step == 0)
    def _():
        carry_ref[...] = jnp.zeros_like(carry_ref)
        lhs2_ref[...] = jnp.zeros_like(lhs2_ref)
        pl.run_scoped(
            lambda stage_ref, sem: _load_weight_bf16(wglu_hbm, wglu_ref, stage_ref, sem, axis=1),
            pltpu.VMEM((2, D_MODEL, D_MODEL // 2), F32), pltpu.SemaphoreType.DMA((2,)))

    @pl.when(step > 0)
    def _():
        lhs2_ref[:, 0:SUBLANES, :] = lhs2_ref[:, rows:rows + SUBLANES, :]

    g = g_ref[...]
    for b in range(nb):
        sh = mod_ref[b, 0:1, :]
        sc = mod_ref[b, 1:2, :]
        h = _rms(x_ref[b], g) * (1.0 + sc) + sh
        for s in range(S5_SLABS):
            hs = h[:, s * LANES:(s + 1) * LANES]
            lhs_ref[s, pl.ds(b, tm, stride=nb), :] = hs
            lhs2_ref[s, pl.ds(nb + b, tm, stride=nb), :] = hs

    for n in range(S5_OCTETS):
        slot = n % 2
        for sl in range(2):
            s = 2 * n + sl
            a = jnp.concatenate([lhs_ref[s], lhs2_ref[s, 0:rows, :]], axis=1).astype(BF16)
            w_ref[slot, :, sl * S5_SLAB_COLS:(sl + 1) * S5_SLAB_COLS] = jnp.dot(
                a, wb_ref[s], preferred_element_type=F32)

        base = n * S5_OCTET_COLS
        for i in range(S5_TILES):
            cr = slice(base + (2 * i) * LANES, base + (2 * i + 1) * LANES)
            ci = slice(base + (2 * i + 1) * LANES, base + (2 * i + 2) * LANES)
            lr, li = lam2_ref[:, cr], lam2_ref[:, ci]
            yr, yi = carry_ref[:, cr], carry_ref[:, ci]
            wr_c = slice((2 * i) * LANES, (2 * i + 1) * LANES)
            wi_c = slice((2 * i + 1) * LANES, (2 * i + 2) * LANES)
            for k in range(rows // SUBLANES):
                rk = slice(k * SUBLANES, (k + 1) * SUBLANES)
                nr = lr * yr - li * yi + w_ref[slot, rk, wr_c]
                ni = lr * yi + li * yr + w_ref[slot, rk, wi_c]
                w_ref[slot, rk, wr_c] = nr
                w_ref[slot, rk, wi_c] = ni
                yr, yi = nr, ni
            carry_ref[:, cr] = yr
            carry_ref[:, ci] = yi

        y_ref[:, n * 2 * LANES:(n + 1) * 2 * LANES] = jnp.dot(
            w_ref[slot].astype(BF16), wc_ref[n], preferred_element_type=F32)

    u = jnp.concatenate([lhs_ref[s] for s in range(S5_SLABS)], axis=1)
    y = y_ref[...] + dskip_ref[...] * u
    z = jax.nn.gelu(y).astype(BF16)
    gl = jnp.dot(z, wglu_ref[...], preferred_element_type=F32)
    mix = gl[:, :D_MODEL] * jax.nn.sigmoid(gl[:, D_MODEL:])
    for s in range(S5_SLABS):
        mix_ref[s] = mix[:, s * LANES:(s + 1) * LANES]
    for b in range(nb):
        yb = jnp.concatenate([mix_ref[s, pl.ds(b, tm, stride=nb), :] for s in range(S5_SLABS)], axis=1)
        o_ref[b] = x_ref[b] + mod_ref[b, 2:3, :] * yb


def _s5_layer(x, mod, g_mix, wb, wc, lam2, d_skip, w_glu):
    nb, seq, d = x.shape
    tm = S5_TM
    rows = nb * tm
    kern = functools.partial(_s5_kernel, tm=tm, nb=nb)
    return pl.pallas_call(
        kern,
        grid=(seq // tm,),
        in_specs=[
            pl.BlockSpec((nb, tm, d), lambda i: (0, i, 0)),
            _const_spec((nb, 6, d)),
            _const_spec((1, d)),
            _const_spec(wb.shape),
            _const_spec(wc.shape),
            _const_spec(lam2.shape),
            _const_spec((1, d)),
            pl.BlockSpec(memory_space=pl.ANY),
        ],
        out_specs=pl.BlockSpec((nb, tm, d), lambda i: (0, i, 0)),
        out_shape=jax.ShapeDtypeStruct(x.shape, F32),
        scratch_shapes=[
            pltpu.VMEM((S5_SLABS, rows, LANES), F32),
            pltpu.VMEM((S5_SLABS, rows + SUBLANES, LANES), F32),
            pltpu.VMEM((2, rows, S5_OCTET_COLS), F32),
            pltpu.VMEM((SUBLANES, S5_COLS), F32),
            pltpu.VMEM((rows, d), F32),
            pltpu.VMEM((S5_SLABS, rows, LANES), F32),
            pltpu.VMEM(w_glu.shape, BF16),
        ],
        compiler_params=_cparams(("arbitrary",)),
        name="s5_layer",
    )(x, mod, g_mix.reshape(1, d), wb, wc, lam2, d_skip.reshape(1, d), w_glu)


def _load_weight_bf16(src_hbm, dst_ref, stage_ref, sem, *, axis):
    chunk = stage_ref.shape[1 + axis]
    n = src_hbm.shape[axis] // chunk

    def window(ref, j):
        idx = [slice(None), slice(None)]
        idx[axis] = pl.ds(j * chunk, chunk)
        return ref.at[tuple(idx)]

    def copy(j):
        return pltpu.make_async_copy(window(src_hbm, j), stage_ref.at[j % 2], sem.at[j % 2])

    copy(0).start()
    for j in range(n):
        if j + 1 < n:
            copy(j + 1).start()
        copy(j).wait()
        window(dst_ref, j)[...] = stage_ref[j % 2].astype(BF16)


def _mlp_kernel(x_ref, mod_ref, g_ref, w1_hbm, w2_hbm, gf_ref, o_ref, w1_ref, w2_ref, *, final, layer):
    @pl.when(pl.program_id(0) == 0)
    def _():
        def load(stage_ref, sem):
            _load_weight_bf16(w1_hbm.at[layer], w1_ref, stage_ref, sem, axis=1)
            _load_weight_bf16(w2_hbm.at[layer], w2_ref, stage_ref, sem, axis=0)
        pl.run_scoped(load, pltpu.VMEM((2, D_MODEL, D_MODEL), F32), pltpu.SemaphoreType.DMA((2,)))

    x = x_ref[...]
    sh = mod_ref[0, 3:4, :]
    sc = mod_ref[0, 4:5, :]
    gt = mod_ref[0, 5:6, :]
    h = (_rms(x, g_ref[...]) * (1.0 + sc) + sh).astype(BF16)
    d_ff = w1_ref.shape[1]
    y = None
    for c0 in range(0, d_ff, MLP_FF_CHUNK):
        a = jnp.dot(h, w1_ref[:, c0:c0 + MLP_FF_CHUNK], preferred_element_type=F32)
        a = jnp.maximum(a, 0.0)
        a = (a * a).astype(BF16)
        part = jnp.dot(a, w2_ref[c0:c0 + MLP_FF_CHUNK, :], preferred_element_type=F32)
        y = part if y is None else y + part
    out = x + gt * y
    if final:
        out = _rms(out, gf_ref[...])
    o_ref[...] = out


def _mlp_layer(x, mod, g_mlp, w1, w2, layer, g_final, final):
    nb, seq, d = x.shape
    tm = MLP_TM
    per_b = seq // tm
    x2 = x.reshape(nb * seq, d)
    out = pl.pallas_call(
        functools.partial(_mlp_kernel, final=final, layer=layer),
        grid=(nb * per_b,),
        in_specs=[
            pl.BlockSpec((tm, d), lambda i: (i, 0)),
            pl.BlockSpec((1, 6, d), lambda i: (i // per_b, 0, 0)),
            _const_spec((1, d)),
            pl.BlockSpec(memory_space=pl.ANY),
            pl.BlockSpec(memory_space=pl.ANY),
            _const_spec((1, d)),
        ],
        out_specs=pl.BlockSpec((tm, d), lambda i: (i, 0)),
        out_shape=jax.ShapeDtypeStruct(x2.shape, F32),
        scratch_shapes=[
            pltpu.VMEM(w1.shape[1:], BF16),
            pltpu.VMEM(w2.shape[1:], BF16),
        ],
        compiler_params=_cparams(("arbitrary",)),
        name="mlp_final" if final else "mlp_layer",
    )(x2, mod, g_mlp.reshape(1, d), w1, w2, g_final.reshape(1, d))
    return out.reshape(nb, seq, d)


def _gla_kernel(x_ref, mod_ref, g_ref, wqkv_ref, wr_ref, wglr_ref, wg2_ref, bg_ref, gn_ref, wout_ref, o_ref,
                st_ref, obuf_ref, *, tm, nb):
    @pl.when(pl.program_id(0) == 0)
    def _():
        st_ref[...] = jnp.zeros_like(st_ref)

    g = g_ref[...]
    o_q, o_k, o_v = 0, GLA_QK, 2 * GLA_QK
    q, k, v, r, la, la2 = [], [], [], [], [], []
    for b in range(nb):
        h = (_rms(x_ref[b], g) * (1.0 + mod_ref[b, 1:2, :]) + mod_ref[b, 0:1, :]).astype(BF16)
        proj = jnp.dot(h, wqkv_ref[...], preferred_element_type=F32)
        q.append((proj[:, o_q:o_k] * (GLA_DK ** -0.5)).astype(BF16))
        k.append(proj[:, o_k:o_v])
        v.append(proj[:, o_v:].astype(BF16))
        r.append(jnp.dot(h, wr_ref[...], preferred_element_type=F32))
        glr = jnp.dot(h, wglr_ref[...], preferred_element_type=F32).astype(BF16)
        pre = jnp.dot(glr, wg2_ref[...], preferred_element_type=F32) + bg_ref[...]
        la_b = (jnp.minimum(pre, 0.0) - jnp.log(1.0 + jnp.exp(-jnp.abs(pre)))) * (1.0 / GLA_GATE_TAU)
        la_hi = la_b.astype(BF16)
        la_lo = (la_b - la_hi.astype(F32)).astype(BF16)
        la.append(la_b)
        la2.append(jnp.concatenate([la_hi, la_lo], axis=1))

    ti = lax.broadcasted_iota(jnp.int32, (CHUNK, CHUNK), 0)
    si = lax.broadcasted_iota(jnp.int32, (CHUNK, CHUNK), 1)
    upper = (si > ti).astype(BF16)

    nchunk = tm // CHUNK
    rows_of = lambda c: slice(c * CHUNK, (c + 1) * CHUNK)
    ks = lambda hh: slice(hh * GLA_DK, (hh + 1) * GLA_DK)
    vs = lambda hh: slice(hh * GLA_DV, (hh + 1) * GLA_DV)
    kd, dec = {}, {}
    for c in range(nchunk):
        for b in range(nb):
            rs = rows_of(c)
            rc2 = jnp.dot(upper, la2[b][rs], preferred_element_type=F32)
            rc = rc2[:, :GLA_QK] + rc2[:, GLA_QK:]
            kd[b, c] = (k[b][rs] * jnp.exp(rc)).astype(BF16)
            dec[b, c] = jnp.exp(jnp.sum(la[b][rs], axis=0, keepdims=True))

    heads = [(b, hh) for b in range(nb) for hh in range(GLA_HEADS)]
    for c in range(nchunk):
        rs = rows_of(c)
        kvt = {(b, hh): lax.dot_general(v[b][rs, vs(hh)], kd[b, c][:, ks(hh)], (((0,), (0,)), ((), ())),
                                        preferred_element_type=F32) for b, hh in heads}
        st = {}
        for b, hh in heads:
            st[b, hh] = dec[b, c][:, ks(hh)] * st_ref[b, hh] + kvt[b, hh]
            st_ref[b, hh] = st[b, hh]
        for b, hh in heads:
            obuf_ref[b, rs, vs(hh)] = lax.dot_general(
                q[b][rs, ks(hh)], st[b, hh].astype(BF16), (((1,), (1,)), ((), ())),
                preferred_element_type=F32)

    for b in range(nb):
        outs = []
        for hh in range(GLA_HEADS):
            oh = obuf_ref[b, :, vs(hh)]
            outs.append(oh * lax.rsqrt(jnp.mean(oh * oh, axis=-1, keepdims=True) + EPS))
        o = jnp.concatenate(outs, axis=1) * gn_ref[...]
        o = (o * (r[b] * jax.nn.sigmoid(r[b]))).astype(BF16)
        y = jnp.dot(o, wout_ref[...], preferred_element_type=F32)
        o_ref[b] = x_ref[b] + mod_ref[b, 2:3, :] * y


def _gla_layer(x, mod, g_mix, w_qkv, w_r, w_glr, w_g2_p, b_gate, g_norm, w_out):
    nb, seq, d = x.shape
    tm = GLA_TM
    return pl.pallas_call(
        functools.partial(_gla_kernel, tm=tm, nb=nb),
        grid=(seq // tm,),
        in_specs=[
            pl.BlockSpec((nb, tm, d), lambda i: (0, i, 0)),
            _const_spec((nb, 6, d)),
            _const_spec((1, d)),
            _const_spec(w_qkv.shape),
            _const_spec(w_r.shape),
            _const_spec(w_glr.shape),
            _const_spec(w_g2_p.shape),
            _const_spec((1, GLA_QK)),
            _const_spec((1, d)),
            _const_spec(w_out.shape),
        ],
        out_specs=pl.BlockSpec((nb, tm, d), lambda i: (0, i, 0)),
        out_shape=jax.ShapeDtypeStruct(x.shape, F32),
        scratch_shapes=[
            pltpu.VMEM((nb, GLA_HEADS, GLA_DV, GLA_DK), F32),
            pltpu.VMEM((nb, tm, d), F32),
        ],
        compiler_params=_cparams(("arbitrary",)),
        name="gla_layer",
    )(x, mod, g_mix.reshape(1, d), w_qkv, w_r, w_glr, w_g2_p, b_gate.reshape(1, GLA_QK), g_norm.reshape(1, d),
      w_out)


def kernel(x, c, w_ada, b_ada, norm_mix, norm_mlp, s5_a_re, s5_a_im, s5_log_dt, s5_b_re, s5_b_im,
           s5_c_re, s5_c_im, s5_d, s5_w_glu, gla_w_in, gla_w_gate2, gla_b_gate, gla_g_norm, gla_w_out,
           w_ff1, w_ff2, norm_final):
    bsz = x.shape[0]
    mod = _adaln(c, w_ada, b_ada).reshape(DEPTH, bsz, 6, D_MODEL)

    wb, wc, lam2 = _s5_prep(s5_a_re[0], s5_a_im[0], s5_log_dt[0], s5_b_re[0], s5_b_im[0], s5_c_re[0], s5_c_im[0])
    x = _s5_layer(x, mod[0], norm_mix[0], wb, wc, lam2, s5_d[0], s5_w_glu[0])
    x = _mlp_layer(x, mod[0], norm_mlp[0], w_ff1, w_ff2, 0, norm_final, False)

    w_in = gla_w_in[0]
    o_g = 2 * GLA_QK + D_MODEL
    o_r = o_g + GLA_GATE_RANK
    w_qkv = w_in[:, :o_g].astype(BF16)
    w_r = w_in[:, o_r:].astype(BF16)
    w_glr = jnp.pad(w_in[:, o_g:o_r], ((0, 0), (0, GLA_GLR_PAD - GLA_GATE_RANK))).astype(BF16)
    w_g2_p = jnp.pad(gla_w_gate2[0], ((0, GLA_GLR_PAD - GLA_GATE_RANK), (0, 0))).astype(BF16)
    x = _gla_layer(x, mod[1], norm_mix[1], w_qkv, w_r, w_glr, w_g2_p, gla_b_gate[0], gla_g_norm[0],
                   gla_w_out[0].astype(BF16))
    x = _mlp_layer(x, mod[1], norm_mlp[1], w_ff1, w_ff2, 1, norm_final, True)
    return x
```
